```python
import jax, jax.numpy as jnp
from jax import lax
import numpy as np

D_MODEL = 4096
BATCH = 8
SEQ = 2048
DEPTH = 1

D_MIX = D_MODEL
HG_WIDTH = D_MIX // 2
HG_HEAD_DIM = 128
HG_HEADS = HG_WIDTH // HG_HEAD_DIM
HG_CHUNK = 64
LRU_WIDTH = D_MIX - HG_WIDTH
LRU_BLOCKS = 16
LRU_BLOCK_DIM = LRU_WIDTH // LRU_BLOCKS
LRU_CONV = 4
LRU_C = 8.0
D_FF = 256 * ((8 * D_MODEL // 3 + 255) // 256)
FFN_CONV = 3
EPS = 1e-6
IN_WIDTHS = (HG_WIDTH, HG_WIDTH, HG_WIDTH, HG_WIDTH, LRU_WIDTH, LRU_WIDTH)
IN_TOTAL = sum(IN_WIDTHS)
IN_SPLIT = tuple(int(v) for v in np.cumsum(IN_WIDTHS)[:-1])

kernel_name = 'hymba_hgrn2_rglru_convffn_block'


def rmsnorm(x, w):
    x32 = x.astype(jnp.float32)
    y = x32 * lax.rsqrt(jnp.mean(x32 * x32, axis=-1, keepdims=True) + EPS)
    return (y * w.astype(jnp.float32)).astype(x.dtype)


def causal_dwconv(x, w, b):
    width = w.shape[0]
    seq = x.shape[1]
    xp = jnp.pad(x, ((0, 0), (width - 1, 0), (0, 0)))
    y = b + xp[:, 0:seq, :] * w[0]
    for j in range(1, width):
        y = y + xp[:, j:j + seq, :] * w[j]
    return y


def hgrn2_chunk_scan(q, k, v, g):
    bsz, seq, nh, dk = q.shape
    dv = v.shape[-1]
    n = seq // HG_CHUNK

    def to_chunks(t):
        return t.reshape(bsz, n, HG_CHUNK, nh, t.shape[-1]).transpose(1, 0, 3, 2, 4)

    causal = jnp.tril(jnp.ones((HG_CHUNK, HG_CHUNK), dtype=bool))[:, :, None]

    def step(state, inp):
        qc, kc, vc, gc = inp
        b = jnp.cumsum(gc, axis=2)
        o_inter = jnp.einsum('bhtk,bhkv->bhtv', qc * jnp.exp(b), state)
        diff = b[:, :, :, None, :] - b[:, :, None, :, :]
        decay = jnp.exp(jnp.where(causal, diff, -jnp.inf))
        scores = jnp.einsum('bhtk,bhtsk,bhsk->bhts', qc, decay, kc)
        o = o_inter + jnp.einsum('bhts,bhsv->bhtv', scores, vc)
        b_last = b[:, :, -1:, :]
        state = (jnp.exp(b_last[:, :, 0, :])[..., None] * state
                 + jnp.einsum('bhsk,bhsv->bhkv', kc * jnp.exp(b_last - b), vc))
        return state, o

    init = jnp.zeros((bsz, nh, dk, dv), jnp.float32)
    _, o = lax.scan(step, init, (to_chunks(q), to_chunks(k), to_chunks(v), to_chunks(g)))
    return o.transpose(1, 0, 3, 2, 4).reshape(bsz, seq, nh, dv)


def hgrn2_group(q_raw, f_raw, i_raw, g_raw, lb, norm_w):
    bsz, seq, _ = q_raw.shape
    q = jax.nn.silu(q_raw.astype(jnp.float32))
    f = lb + (1.0 - lb) * jax.nn.sigmoid(f_raw.astype(jnp.float32))
    k = 1.0 - f
    logf = jnp.log(f)
    v = i_raw.astype(jnp.float32)
    shp = (bsz, seq, HG_HEADS, HG_HEAD_DIM)
    o = hgrn2_chunk_scan(q.reshape(shp), k.reshape(shp), v.reshape(shp), logf.reshape(shp))
    o = rmsnorm(o, norm_w.reshape(HG_HEADS, HG_HEAD_DIM))
    return o.reshape(bsz, seq, HG_WIDTH) * jax.nn.silu(g_raw.astype(jnp.float32))


def _lin_combine(c1, c2):
    a1, b1 = c1
    a2, b2 = c2
    return a1 * a2, a2 * b1 + b2


def rglru_group(x_raw, y_raw, conv_w, conv_b, wa, ba, wx, bx, lam):
    bsz, seq, _ = x_raw.shape
    xb = causal_dwconv(x_raw, conv_w, conv_b).astype(jnp.float32)
    xblk = xb.reshape(bsz, seq, LRU_BLOCKS, LRU_BLOCK_DIM)
    r = jax.nn.sigmoid(jnp.einsum('bsnd,nde->bsne', xblk, wa).reshape(bsz, seq, LRU_WIDTH) + ba)
    i = jax.nn.sigmoid(jnp.einsum('bsnd,nde->bsne', xblk, wx).reshape(bsz, seq, LRU_WIDTH) + bx)
    log_a = -LRU_C * r * jax.nn.softplus(-lam.astype(jnp.float32))
    a = jnp.exp(log_a)
    mult = jnp.sqrt(-jnp.expm1(2.0 * log_a))
    mult = mult.at[:, 0].set(1.0)
    u = xb * i * mult
    _, h = lax.associative_scan(_lin_combine, (a, u), axis=1)
    return h * jax.nn.gelu(y_raw.astype(jnp.float32))


def setup_inputs(seed: int = 0) -> dict:
    key = jax.random.key(seed)
    ks = jax.random.split(key, 20)
    f32 = jnp.float32
    nrm = lambda k, shp, s: (jax.random.normal(k, shp, f32) * s)
    x = jax.random.normal(ks[0], (BATCH, SEQ, D_MODEL), f32)
    ln1_w = 1.0 + nrm(ks[1], (DEPTH, D_MODEL), 0.02)
    w_in = nrm(ks[2], (DEPTH, D_MODEL, IN_TOTAL), D_MODEL ** -0.5)
    lb_gamma = nrm(ks[3], (DEPTH + 1, HG_WIDTH), 0.5)
    hg_norm_w = 1.0 + nrm(ks[4], (DEPTH, HG_WIDTH), 0.02)
    lru_conv_w = nrm(ks[5], (DEPTH, LRU_CONV, LRU_WIDTH), LRU_CONV ** -0.5)
    lru_conv_b = nrm(ks[6], (DEPTH, LRU_WIDTH), 0.02)
    lru_wa = nrm(ks[7], (DEPTH, LRU_BLOCKS, LRU_BLOCK_DIM, LRU_BLOCK_DIM), LRU_BLOCK_DIM ** -0.5)
    lru_ba = nrm(ks[8], (DEPTH, LRU_WIDTH), 0.1)
    lru_wx = nrm(ks[9], (DEPTH, LRU_BLOCKS, LRU_BLOCK_DIM, LRU_BLOCK_DIM), LRU_BLOCK_DIM ** -0.5)
    lru_bx = nrm(ks[10], (DEPTH, LRU_WIDTH), 0.1)
    a_c = jax.random.uniform(ks[11], (DEPTH, LRU_WIDTH), f32, 0.9, 0.999)
    a0 = a_c ** (1.0 / LRU_C)
    lru_lambda = jnp.log(a0) - jnp.log1p(-a0)
    lru_norm_w = 1.0 + nrm(ks[12], (DEPTH, LRU_WIDTH), 0.02)
    w_out = nrm(ks[13], (DEPTH, D_MIX, D_MODEL), D_MIX ** -0.5)
    ln2_w = 1.0 + nrm(ks[14], (DEPTH, D_MODEL), 0.02)
    ffn_w_up = nrm(ks[15], (DEPTH, D_MODEL, 2 * D_FF), D_MODEL ** -0.5)
    ffn_conv_w = nrm(ks[16], (DEPTH, FFN_CONV, 2 * D_FF), FFN_CONV ** -0.5)
    ffn_conv_b = nrm(ks[17], (DEPTH, 2 * D_FF), 0.02)
    ffn_w_down = nrm(ks[18], (DEPTH, D_FF, D_MODEL), D_FF ** -0.5)
    final_norm_w = 1.0 + nrm(ks[19], (D_MODEL,), 0.02)
    return {'x': x, 'ln1_w': ln1_w, 'w_in': w_in, 'lb_gamma': lb_gamma,
            'hg_norm_w': hg_norm_w, 'lru_conv_w': lru_conv_w, 'lru_conv_b': lru_conv_b,
            'lru_wa': lru_wa, 'lru_ba': lru_ba, 'lru_wx': lru_wx, 'lru_bx': lru_bx,
            'lru_lambda': lru_lambda, 'lru_norm_w': lru_norm_w, 'w_out': w_out,
            'ln2_w': ln2_w, 'ffn_w_up': ffn_w_up, 'ffn_conv_w': ffn_conv_w,
            'ffn_conv_b': ffn_conv_b, 'ffn_w_down': ffn_w_down, 'final_norm_w': final_norm_w}


def reference(x, ln1_w, w_in, lb_gamma, hg_norm_w, lru_conv_w, lru_conv_b, lru_wa, lru_ba,
              lru_wx, lru_bx, lru_lambda, lru_norm_w, w_out, ln2_w, ffn_w_up, ffn_conv_w,
              ffn_conv_b, ffn_w_down, final_norm_w):
    lb_all = jnp.cumsum(jax.nn.softmax(lb_gamma.astype(jnp.float32), axis=0), axis=0)
    h = x
    for l in range(DEPTH):
        hn = rmsnorm(h, ln1_w[l])
        proj = jnp.einsum('bsd,de->bse', hn, w_in[l])
        q_r, f_r, i_r, g_r, x_r, y_r = jnp.split(proj, IN_SPLIT, axis=-1)
        o_hg = hgrn2_group(q_r, f_r, i_r, g_r, lb_all[l], hg_norm_w[l])
        o_lru = rglru_group(x_r, y_r, lru_conv_w[l], lru_conv_b[l], lru_wa[l], lru_ba[l],
                            lru_wx[l], lru_bx[l], lru_lambda[l])
        o_lru = rmsnorm(o_lru, lru_norm_w[l])
        mix = jnp.concatenate([o_hg, o_lru], axis=-1).astype(h.dtype)
        h = h + jnp.einsum('bse,ed->bsd', mix, w_out[l])
        hn = rmsnorm(h, ln2_w[l])
        up = jnp.einsum('bsd,df->bsf', hn, ffn_w_up[l])
        up = causal_dwconv(up, ffn_conv_w[l], ffn_conv_b[l])
        gate, val = jnp.split(up, [D_FF], axis=-1)
        h = h + jnp.einsum('bsf,fd->bsd', jax.nn.silu(gate) * val, ffn_w_down[l])
    return rmsnorm(h, final_norm_w)
```

```python
import functools

import jax
import jax.numpy as jnp
from jax import lax
from jax.experimental import pallas as pl
from jax.experimental.pallas import tpu as pltpu

EPS = 1e-6
LRU_C = 8.0
HEAD_DIM = 128
HG_CHUNK = 128
LANES = 128
SUBLANES = 8
VMEM_LIMIT = 56 * 1024 * 1024

F32 = jnp.float32
BF16 = jnp.bfloat16


def _cparams(sem):
    return pltpu.CompilerParams(dimension_semantics=sem, vmem_limit_bytes=VMEM_LIMIT)


def _tile(dim, pref, unit):
    if dim <= pref:
        return dim
    t = (pref // unit) * unit
    while dim % t:
        t -= unit
    assert t > 0
    return t


def _sigmoid(x):
    return 1.0 / (1.0 + jnp.exp(-x))


def _dot_nt(a, b):
    return lax.dot_general(a, b, (((1,), (1,)), ((), ())), preferred_element_type=F32)


def _dot_tn(a, b):
    return lax.dot_general(a, b, (((0,), (0,)), ((), ())), preferred_element_type=F32)


def _rmsnorm_body(x_ref, w_ref, o_ref):
    x = x_ref[...].astype(F32)
    ms = jnp.mean(x * x, axis=-1, keepdims=True)
    o_ref[...] = (x * lax.rsqrt(ms + EPS) * w_ref[...]).astype(o_ref.dtype)


def _rmsnorm(x, w, out_dtype, tm):
    m, d = x.shape
    tm = _tile(m, tm, SUBLANES)
    return pl.pallas_call(
        _rmsnorm_body,
        grid=(m // tm,),
        in_specs=[pl.BlockSpec((tm, d), lambda i: (i, 0)),
                  pl.BlockSpec((1, d), lambda i: (0, 0))],
        out_specs=pl.BlockSpec((tm, d), lambda i: (i, 0)),
        out_shape=jax.ShapeDtypeStruct((m, d), out_dtype),
        compiler_params=_cparams(("parallel",)),
        name="rmsnorm",
    )(x, w.reshape(1, d).astype(F32))


def _matmul_body(x_ref, w_ref, o_ref):
    o_ref[...] = jnp.dot(x_ref[...], w_ref[...], preferred_element_type=F32).astype(o_ref.dtype)


def _matmul(x, w, out_dtype, tm, tn):
    m, k = x.shape
    n = w.shape[1]
    tm, tn = _tile(m, tm, SUBLANES), _tile(n, tn, LANES)
    return pl.pallas_call(
        _matmul_body,
        grid=(m // tm, n // tn),
        in_specs=[pl.BlockSpec((tm, k), lambda i, j: (i, 0)),
                  pl.BlockSpec((k, tn), lambda i, j: (0, j))],
        out_specs=pl.BlockSpec((tm, tn), lambda i, j: (i, j)),
        out_shape=jax.ShapeDtypeStruct((m, n), out_dtype),
        compiler_params=_cparams(("parallel", "arbitrary")),
        name="in_proj",
    )(x, w)


def _hgrn_levels(c):
    halves = []
    h = c // 2
    while h >= 1:
        halves.append(h)
        h //= 2
    return halves


def _boundary_rows(b, half):
    c, d = b.shape
    blk = 2 * half
    if half >= SUBLANES:
        parts = []
        for r0 in range(0, c, blk):
            p = r0 + half - 1
            parts.append(jnp.broadcast_to(b[p:p + 1, :], (blk, d)))
        return parts[0] if len(parts) == 1 else jnp.concatenate(parts, axis=0)
    b3 = b.reshape(c // SUBLANES, SUBLANES, d)
    row = lax.broadcasted_iota(jnp.int32, b3.shape, 1)
    out = None
    for r0 in range(0, SUBLANES, blk):
        p = r0 + half - 1
        cand = jnp.broadcast_to(b3[:, p:p + 1, :], b3.shape)
        out = cand if out is None else jnp.where(row >= r0, cand, out)
    return out.reshape(c, d)


def _hgrn_body(q_ref, f_ref, v_ref, g_ref, lb_ref, nw_ref, o_ref, st_ref, *, n_heads, n_chunks):
    c = HG_CHUNK
    dk = HEAD_DIM
    halves = _hgrn_levels(c)

    @pl.when(pl.program_id(1) == 0)
    def _():
        st_ref[...] = jnp.zeros_like(st_ref)

    ti = lax.broadcasted_iota(jnp.int32, (c, c), 0)
    si = lax.broadcasted_iota(jnp.int32, (c, c), 1)
    x = ti ^ si
    lvl = jnp.where(x == 0, -1, 0)
    for j in range(1, len(halves)):
        lvl = lvl + (x >= (1 << j)).astype(jnp.int32)
    lvl = jnp.where(si > ti, -2, lvl)
    tril = (si <= ti).astype(BF16)
    odd = (lax.broadcasted_iota(jnp.int32, (c, dk), 0) & 1) == 1

    def head_chunk(idx, carry):
        ci = idx // n_heads
        hi = idx % n_heads
        rows = pl.ds(pl.multiple_of(ci * c, c), c)
        cols = pl.ds(pl.multiple_of(hi * dk, dk), dk)
        qr = q_ref[rows, cols].astype(F32)
        fr = f_ref[rows, cols].astype(F32)
        v = v_ref[rows, cols]
        gr = g_ref[rows, cols].astype(F32)
        lb = lb_ref[:, cols]
        q = qr * _sigmoid(qr)
        f = lb + (1.0 - lb) * _sigmoid(fr)
        k = 1.0 - f
        g = jnp.log(f)
        g_hi = g.astype(BF16)
        g_lo = (g - g_hi.astype(F32)).astype(BF16)
        b = (jnp.dot(tril, g_hi, preferred_element_type=F32)
             + jnp.dot(tril, g_lo, preferred_element_type=F32))
        b_last = b[c - 1:c, :]

        st = st_ref[hi]
        inter = _dot_nt((q * jnp.exp(b)).astype(BF16), st.astype(BF16))

        kb = k.astype(BF16)
        scores = jnp.where(lvl == -1, _dot_nt(q.astype(BF16), kb), 0.0)
        for li, half in enumerate(halves):
            level = len(halves) - 1 - li
            if half == 1:
                e = jnp.where(odd, f, 1.0)
            else:
                e = jnp.exp(-jnp.abs(b - _boundary_rows(b, half)))
            s_l = _dot_nt((q * e).astype(BF16), (k * e).astype(BF16))
            scores = jnp.where(lvl == level, s_l, scores)
        o = inter + jnp.dot(scores.astype(BF16), v, preferred_element_type=F32)

        k_end = (k * jnp.exp(b_last - b)).astype(BF16)
        st_ref[hi] = jnp.exp(b_last) * st + _dot_tn(v, k_end)

        ms = jnp.mean(o * o, axis=-1, keepdims=True)
        o = o * lax.rsqrt(ms + EPS) * nw_ref[:, cols]
        o_ref[rows, cols] = (o * (gr * _sigmoid(gr))).astype(o_ref.dtype)
        return carry

    lax.fori_loop(0, n_chunks * n_heads, head_chunk, 0)


def _hgrn(proj, lb, norm_w, batch, seq, width, ts):
    t = proj.shape[0]
    n_heads = width // HEAD_DIM
    ts = _tile(seq, ts, HG_CHUNK)
    n_chunks = ts // HG_CHUNK
    steps = seq // ts
    row_map = lambda col: (lambda b, s: (b * steps + s, col))
    body = functools.partial(_hgrn_body, n_heads=n_heads, n_chunks=n_chunks)
    return pl.pallas_call(
        body,
        grid=(batch, steps),
        in_specs=[pl.BlockSpec((ts, width), row_map(0)),
                  pl.BlockSpec((ts, width), row_map(1)),
                  pl.BlockSpec((ts, width), row_map(2)),
                  pl.BlockSpec((ts, width), row_map(3)),
                  pl.BlockSpec((1, width), lambda b, s: (0, 0)),
                  pl.BlockSpec((1, width), lambda b, s: (0, 0))],
        out_specs=pl.BlockSpec((ts, width), row_map(0)),
        out_shape=jax.ShapeDtypeStruct((t, width), BF16),
        scratch_shapes=[pltpu.VMEM((n_heads, HEAD_DIM, HEAD_DIM), F32)],
        compiler_params=_cparams(("parallel", "arbitrary")),
        name="hgrn2",
    )(proj, proj, proj, proj, lb, norm_w)


def _shift_rows(x, d, fill):
    rolled = pltpu.roll(x, d, axis=0)
    row = lax.broadcasted_iota(jnp.int32, x.shape, 0)
    return jnp.where(row < d, fill, rolled)


def _lru_body(x_ref, y_ref, cw_ref, cb_ref, wg_ref, ba_ref, bx_ref, lam_ref, nw_ref,
              o_ref, halo_ref, h_ref, *, n_blocks, conv_w):
    tl, width = x_ref.shape
    bd = width // n_blocks
    first = pl.program_id(1) == 0

    @pl.when(first)
    def _():
        halo_ref[...] = jnp.zeros_like(halo_ref)
        h_ref[...] = jnp.zeros_like(h_ref)

    x = x_ref[...].astype(F32)
    xcat = jnp.concatenate([halo_ref[...], x], axis=0)
    halo_ref[...] = x[tl - SUBLANES:, :]
    xb = cb_ref[...] + x * cw_ref[conv_w - 1:conv_w, :]
    for j in range(conv_w - 1):
        d = conv_w - 1 - j
        xb = xb + pltpu.roll(xcat, d, axis=0)[SUBLANES:, :] * cw_ref[j:j + 1, :]

    xbb = xb.astype(BF16)
    pre = [jnp.dot(xbb[:, n * bd:(n + 1) * bd], wg_ref[n], preferred_element_type=F32)
           for n in range(n_blocks)]
    pre_a = jnp.concatenate([p[:, :bd] for p in pre], axis=1)
    pre_x = jnp.concatenate([p[:, bd:] for p in pre], axis=1)
    r = _sigmoid(pre_a + ba_ref[...])
    gate_i = _sigmoid(pre_x + bx_ref[...])
    lam = lam_ref[...]
    softplus = jnp.maximum(-lam, 0.0) + jnp.log(1.0 + jnp.exp(-jnp.abs(lam)))
    log_a = (-LRU_C) * r * softplus
    a = jnp.exp(log_a)
    mult = jnp.sqrt(jnp.tanh(-log_a) * (1.0 + a * a))
    row = lax.broadcasted_iota(jnp.int32, (tl, width), 0)
    mult = jnp.where(jnp.logical_and(first, row == 0), 1.0, mult)
    u = xb * gate_i * mult

    d = 1
    while d < tl:
        u = a * _shift_rows(u, d, 0.0) + u
        a = a * _shift_rows(a, d, 1.0)
        d *= 2
    h = u + a * h_ref[0:1, :]
    h_ref[...] = jnp.broadcast_to(h[tl - 1:tl, :], h_ref.shape)

    out = h * jax.nn.gelu(y_ref[...].astype(F32))
    ms = jnp.mean(out * out, axis=-1, keepdims=True)
    o_ref[...] = (out * lax.rsqrt(ms + EPS) * nw_ref[...]).astype(o_ref.dtype)


def _lru(proj, conv_w, conv_b, w_gate, ba, bx, lam, norm_w, batch, seq, width, col0, tl):
    t = proj.shape[0]
    tl = _tile(seq, tl, SUBLANES)
    steps = seq // tl
    n_blocks = w_gate.shape[0]
    row_map = lambda col: (lambda b, s: (b * steps + s, col))
    vec = lambda r: pl.BlockSpec((r, width), lambda b, s: (0, 0))
    body = functools.partial(_lru_body, n_blocks=n_blocks, conv_w=conv_w.shape[0])
    return pl.pallas_call(
        body,
        grid=(batch, steps),
        in_specs=[pl.BlockSpec((tl, width), row_map(col0)),
                  pl.BlockSpec((tl, width), row_map(col0 + 1)),
                  vec(conv_w.shape[0]), vec(1),
                  pl.BlockSpec(w_gate.shape, lambda b, s: (0, 0, 0)),
                  vec(1), vec(1), vec(1), vec(1)],
        out_specs=pl.BlockSpec((tl, width), row_map(0)),
        out_shape=jax.ShapeDtypeStruct((t, width), BF16),
        scratch_shapes=[pltpu.VMEM((SUBLANES, width), F32), pltpu.VMEM((SUBLANES, width), F32)],
        compiler_params=_cparams(("parallel", "arbitrary")),
        name="rglru",
    )(proj, proj, conv_w, conv_b, w_gate, ba, bx, lam, norm_w)


def _out_proj_body(a_ref, b_ref, wa_ref, wb_ref, x_ref, o_ref):
    acc = jnp.dot(a_ref[...], wa_ref[...], preferred_element_type=F32)
    acc = acc + jnp.dot(b_ref[...], wb_ref[...], preferred_element_type=F32)
    o_ref[...] = x_ref[...] + acc


def _out_proj(o_hg, o_lru, w_out, x, tm, tn):
    m, ka = o_hg.shape
    kb = o_lru.shape[1]
    n = w_out.shape[1]
    tm, tn = _tile(m, tm, SUBLANES), _tile(n, tn, LANES)
    assert ka == kb
    return pl.pallas_call(
        _out_proj_body,
        grid=(m // tm, n // tn),
        in_specs=[pl.BlockSpec((tm, ka), lambda i, j: (i, 0)),
                  pl.BlockSpec((tm, kb), lambda i, j: (i, 0)),
                  pl.BlockSpec((ka, tn), lambda i, j: (0, j)),
                  pl.BlockSpec((kb, tn), lambda i, j: (1, j)),
                  pl.BlockSpec((tm, tn), lambda i, j: (i, j))],
        out_specs=pl.BlockSpec((tm, tn), lambda i, j: (i, j)),
        out_shape=jax.ShapeDtypeStruct((m, n), F32),
        compiler_params=_cparams(("parallel", "arbitrary")),
        name="out_proj",
    )(o_hg, o_lru, w_out, w_out, x)


def _ffn_up_body(x_ref, wg_ref, wv_ref, cg_ref, cv_ref, bg_ref, bv_ref, o_ref, *, rows_per_dot, conv_w):
    seq = x_ref.shape[0]
    tf = wg_ref.shape[1]
    w = jnp.concatenate([wg_ref[...], wv_ref[...]], axis=1)
    cw = jnp.concatenate([cg_ref[...], cv_ref[...]], axis=1)
    bias = jnp.concatenate([bg_ref[...], bv_ref[...]], axis=1)
    tail = jnp.zeros((SUBLANES, 2 * tf), F32)
    for r0 in range(0, seq, rows_per_dot):
        up = jnp.dot(x_ref[r0:r0 + rows_per_dot, :], w, preferred_element_type=F32)
        cat = jnp.concatenate([tail, up], axis=0)
        tail = up[rows_per_dot - SUBLANES:, :]
        y = bias + up * cw[conv_w - 1:conv_w, :]
        for j in range(conv_w - 1):
            d = conv_w - 1 - j
            y = y + pltpu.roll(cat, d, axis=0)[SUBLANES:, :] * cw[j:j + 1, :]
        gate = y[:, :tf]
        val = y[:, tf:]
        o_ref[r0:r0 + rows_per_dot, :] = (gate * _sigmoid(gate) * val).astype(o_ref.dtype)


def _ffn_up(hn, w_up, conv_w, conv_b, batch, seq, d_ff, tf, rows_per_dot):
    t, d = hn.shape
    tf = _tile(d_ff, tf, LANES)
    nf = d_ff // tf
    rows_per_dot = _tile(seq, rows_per_dot, SUBLANES)
    cw = conv_w.shape[0]
    body = functools.partial(_ffn_up_body, rows_per_dot=rows_per_dot, conv_w=cw)
    return pl.pallas_call(
        body,
        grid=(batch, nf),
        in_specs=[pl.BlockSpec((seq, d), lambda b, j: (b, 0), pipeline_mode=pl.Buffered(1)),
                  pl.BlockSpec((d, tf), lambda b, j: (0, j)),
                  pl.BlockSpec((d, tf), lambda b, j: (0, j + nf)),
                  pl.BlockSpec((cw, tf), lambda b, j: (0, j)),
                  pl.BlockSpec((cw, tf), lambda b, j: (0, j + nf)),
                  pl.BlockSpec((1, tf), lambda b, j: (0, j)),
                  pl.BlockSpec((1, tf), lambda b, j: (0, j + nf))],
        out_specs=pl.BlockSpec((seq, tf), lambda b, j: (b, j)),
        out_shape=jax.ShapeDtypeStruct((t, d_ff), BF16),
        compiler_params=_cparams(("parallel", "arbitrary")),
        name="ffn_up",
    )(hn, w_up, w_up, conv_w, conv_w, conv_b, conv_b)


def _ffn_down_body(a_ref, w_ref, h_ref, o_ref):
    o_ref[...] = h_ref[...] + jnp.dot(a_ref[...], w_ref[...], preferred_element_type=F32)


def _ffn_down(act, w_down, h, tm, tn):
    m, k = act.shape
    n = w_down.shape[1]
    tm, tn = _tile(m, tm, SUBLANES), _tile(n, tn, LANES)
    return pl.pallas_call(
        _ffn_down_body,
        grid=(n // tn, m // tm),
        in_specs=[pl.BlockSpec((tm, k), lambda j, i: (i, 0)),
                  pl.BlockSpec((k, tn), lambda j, i: (0, j), pipeline_mode=pl.Buffered(1)),
                  pl.BlockSpec((tm, tn), lambda j, i: (i, j))],
        out_specs=pl.BlockSpec((tm, tn), lambda j, i: (i, j)),
        out_shape=jax.ShapeDtypeStruct((m, n), F32),
        compiler_params=_cparams(("parallel", "arbitrary")),
        name="ffn_down",
    )(act, w_down, h)


def kernel(x, ln1_w, w_in, lb_gamma, hg_norm_w, lru_conv_w, lru_conv_b, lru_wa, lru_ba, lru_wx, lru_bx,
           lru_lambda, lru_norm_w, w_out, ln2_w, ffn_w_up, ffn_conv_w, ffn_conv_b, ffn_w_down,
           final_norm_w):
    batch, seq, d = x.shape
    depth = ln1_w.shape[0]
    hg_w = hg_norm_w.shape[1]
    lru_w = lru_lambda.shape[1]
    d_ff = ffn_w_down.shape[1]
    assert hg_w == lru_w and w_in.shape[2] == 4 * hg_w + 2 * lru_w
    t = batch * seq
    row = lambda v: v.reshape(1, -1).astype(F32)

    lb_all = jnp.cumsum(jax.nn.softmax(lb_gamma.astype(F32), axis=0), axis=0)
    h = x.reshape(t, d)
    for l in range(depth):
        hn = _rmsnorm(h, ln1_w[l], BF16, tm=512)
        proj = _matmul(hn, w_in[l].astype(BF16), BF16, tm=1024, tn=1024)
        o_hg = _hgrn(proj, row(lb_all[l]), row(hg_norm_w[l]), batch, seq, hg_w, ts=512)
        w_gate = jnp.concatenate([lru_wa[l], lru_wx[l]], axis=-1).astype(BF16)
        o_lru = _lru(proj, lru_conv_w[l].astype(F32), row(lru_conv_b[l]), w_gate, row(lru_ba[l]),
                     row(lru_bx[l]), row(lru_lambda[l]), row(lru_norm_w[l]), batch, seq, lru_w,
                     col0=4, tl=256)
        h = _out_proj(o_hg, o_lru, w_out[l].astype(BF16), h, tm=512, tn=1024)
        hn = _rmsnorm(h, ln2_w[l], BF16, tm=512)
        act = _ffn_up(hn, ffn_w_up[l].astype(BF16), ffn_conv_w[l].astype(F32), row(ffn_conv_b[l]),
                      batch, seq, d_ff, tf=256, rows_per_dot=256)
        h = _ffn_down(act, ffn_w_down[l].astype(BF16), h, tm=256, tn=1024)
    out = _rmsnorm(h, final_norm_w, x.dtype, tm=512)
    return out.reshape(batch, seq, d)
```

```python
import functools

import jax
import jax.numpy as jnp
from jax import lax
from jax.experimental import pallas as pl
from jax.experimental.pallas import tpu as pltpu

EPS = 1e-6
LRU_C = 8.0
HEAD_DIM = 128
HG_CHUNK = 128
HG_GROUP = 4
LANES = 128
SUBLANES = 8
VMEM_LIMIT = 56 * 1024 * 1024

F32 = jnp.float32
BF16 = jnp.bfloat16


def _cparams(sem):
    return pltpu.CompilerParams(dimension_semantics=sem, vmem_limit_bytes=VMEM_LIMIT)


def _tile(dim, pref, unit):
    if dim <= pref:
        return dim
    t = (pref // unit) * unit
    while dim % t:
        t -= unit
    assert t > 0
    return t


def _neg_abs(x):
    bits = lax.bitcast_convert_type(x, jnp.int32) | jnp.int32(-2 ** 31)
    return lax.bitcast_convert_type(bits, F32)


def _silu(x):
    hx = 0.5 * x
    return hx + hx * jnp.tanh(hx)


def _sigmoid(x):
    return 0.5 + 0.5 * jnp.tanh(0.5 * x)


def _dot_nt(a, b):
    return lax.dot_general(a, b, (((1,), (1,)), ((), ())), preferred_element_type=F32)


def _dot_tn(a, b):
    return lax.dot_general(a, b, (((0,), (0,)), ((), ())), preferred_element_type=F32)


def _rmsnorm_body(x_ref, w_ref, o_ref):
    x = x_ref[...].astype(F32)
    ms = jnp.mean(x * x, axis=-1, keepdims=True)
    o_ref[...] = (x * lax.rsqrt(ms + EPS) * w_ref[...]).astype(o_ref.dtype)


def _rmsnorm(x, w, out_dtype, tm):
    m, d = x.shape
    tm = _tile(m, tm, SUBLANES)
    return pl.pallas_call(
        _rmsnorm_body,
        grid=(m // tm,),
        in_specs=[pl.BlockSpec((tm, d), lambda i: (i, 0)),
                  pl.BlockSpec((1, d), lambda i: (0, 0))],
        out_specs=pl.BlockSpec((tm, d), lambda i: (i, 0)),
        out_shape=jax.ShapeDtypeStruct((m, d), out_dtype),
        compiler_params=_cparams(("parallel",)),
        name="rmsnorm",
    )(x, w.reshape(1, d).astype(F32))


def _matmul_body(x_ref, w_ref, o_ref):
    o_ref[...] = jnp.dot(x_ref[...], w_ref[...], preferred_element_type=F32).astype(o_ref.dtype)


def _matmul(x, w, out_dtype, tm, tn):
    m, k = x.shape
    n = w.shape[1]
    tm, tn = _tile(m, tm, SUBLANES), _tile(n, tn, LANES)
    return pl.pallas_call(
        _matmul_body,
        grid=(m // tm, n // tn),
        in_specs=[pl.BlockSpec((tm, k), lambda i, j: (i, 0)),
                  pl.BlockSpec((k, tn), lambda i, j: (0, j))],
        out_specs=pl.BlockSpec((tm, tn), lambda i, j: (i, j)),
        out_shape=jax.ShapeDtypeStruct((m, n), out_dtype),
        compiler_params=_cparams(("parallel", "arbitrary")),
        name="in_proj",
    )(x, w)


def _hgrn_levels(c):
    halves = []
    h = c // 2
    while h >= 1:
        halves.append(h)
        h //= 2
    return halves


def _boundary_rows_in_vreg(b, half):
    c, d = b.shape
    blk = 2 * half
    b3 = b.reshape(c // SUBLANES, SUBLANES, d)
    row = lax.broadcasted_iota(jnp.int32, b3.shape, 1)
    out = None
    for r0 in range(0, SUBLANES, blk):
        p = r0 + half - 1
        cand = jnp.broadcast_to(b3[:, p:p + 1, :], b3.shape)
        out = cand if out is None else jnp.where(row >= r0, cand, out)
    return out.reshape(c, d)


def _hgrn_body(q_ref, f_ref, v_ref, g_ref, lb_ref, nw_ref, o_ref, st_ref, lvl_ref, *, n_heads, n_chunks):
    c = HG_CHUNK
    dk = HEAD_DIM
    halves = _hgrn_levels(c)
    n_lv = len(halves)
    nv = c // SUBLANES
    group = HG_GROUP if n_heads % HG_GROUP == 0 else 1
    n_groups = n_heads // group

    @pl.when(pl.program_id(1) == 0)
    def _():
        st_ref[...] = jnp.zeros_like(st_ref)

    ti = lax.broadcasted_iota(jnp.int32, (c, c), 0)
    si = lax.broadcasted_iota(jnp.int32, (c, c), 1)
    x = ti ^ si
    lvl = jnp.where(x == 0, -1, 0)
    for j in range(1, n_lv):
        lvl = lvl + (x >= (1 << j)).astype(jnp.int32)
    lvl_ref[...] = jnp.where(si > ti, -2, lvl)
    tril2 = jnp.concatenate([(si <= ti).astype(BF16)] * 2, axis=1)
    odd = (lax.broadcasted_iota(jnp.int32, (c, dk), 0) & 1) == 1

    def head_group(idx, carry):
        ci = idx // n_groups
        gi = idx % n_groups
        rows = pl.ds(pl.multiple_of(ci * c, c), c)
        heads = [gi * group + j for j in range(group)]
        cols = [pl.ds(pl.multiple_of(h * dk, dk), dk) for h in heads]

        qs, fs, ks, vs, b2s, kbs = [], [], [], [], [], []
        for cl in cols:
            lb = lb_ref[:, cl]
            q = _silu(q_ref[rows, cl].astype(F32))
            f = lb + (1.0 - lb) * _sigmoid(f_ref[rows, cl].astype(F32))
            k = 1.0 - f
            g2 = jnp.log2(f)
            g_hi = g2.astype(BF16)
            g_lo = (g2 - g_hi.astype(F32)).astype(BF16)
            b2s.append(jnp.dot(tril2, jnp.concatenate([g_hi, g_lo], axis=0), preferred_element_type=F32))
            qs.append(q)
            fs.append(f)
            ks.append(k)
            kbs.append(k.astype(BF16))
            vs.append(v_ref[rows, cl])

        scs = []
        for q, f, kb in zip(qs, fs, kbs):
            s_d = _dot_nt(q.astype(BF16), kb)
            s_1 = _dot_nt((q * jnp.where(odd, f, 1.0)).astype(BF16), kb)
            sc = []
            for r in range(nv):
                rs = slice(r * SUBLANES, (r + 1) * SUBLANES)
                lv = lvl_ref[rs, :]
                sc.append(jnp.where(lv == 0, s_1[rs, :], jnp.where(lv == -1, s_d[rs, :], 0.0)))
            scs.append(sc)

        inters = []
        for h, q, k, v, b2 in zip(heads, qs, ks, vs, b2s):
            b_last = b2[c - 1:c, :]
            st = st_ref[h]
            inters.append(_dot_nt((q * jnp.exp2(b2)).astype(BF16), st.astype(BF16)))
            k_end = (k * jnp.exp2(b_last - b2)).astype(BF16)
            st_ref[h] = jnp.exp2(b_last) * st + _dot_tn(v, k_end)

        for li, half in enumerate(halves[:-1]):
            level = n_lv - 1 - li
            blk = 2 * half
            prods = []
            for q, k, b2 in zip(qs, ks, b2s):
                if half >= SUBLANES:
                    lhs, rhs = [], []
                    for r0 in range(0, c, blk):
                        bp = b2[r0 + half - 1:r0 + half, :]
                        lo = slice(r0, r0 + half)
                        up = slice(r0 + half, r0 + blk)
                        lhs.append(q[up, :] * jnp.exp2(b2[up, :] - bp))
                        rhs.append(k[lo, :] * jnp.exp2(bp - b2[lo, :]))
                        rhs.append(k[up, :])
                    lhs = lhs[0] if len(lhs) == 1 else jnp.concatenate(lhs, axis=0)
                    prods.append(_dot_nt(lhs.astype(BF16), jnp.concatenate(rhs, axis=0).astype(BF16)))
                else:
                    e = jnp.exp2(_neg_abs(b2 - _boundary_rows_in_vreg(b2, half)))
                    prods.append(_dot_nt((q * e).astype(BF16), (k * e).astype(BF16)))
            if half >= SUBLANES:
                row_of = [r // SUBLANES for r0 in range(0, c, blk) for r in range(r0 + half, r0 + blk, SUBLANES)]
            else:
                row_of = list(range(nv))
            for sc, s_l in zip(scs, prods):
                for i, r in enumerate(row_of):
                    rs = slice(r * SUBLANES, (r + 1) * SUBLANES)
                    piece = s_l[i * SUBLANES:(i + 1) * SUBLANES, :]
                    sc[r] = jnp.where(lvl_ref[rs, :] == level, piece, sc[r])

        outs = []
        for sc, v, inter in zip(scs, vs, inters):
            scores = jnp.concatenate(sc, axis=0).astype(BF16)
            outs.append(inter + jnp.dot(scores, v, preferred_element_type=F32))
        for o, cl in zip(outs, cols):
            ms = jnp.mean(o * o, axis=-1, keepdims=True)
            o = o * lax.rsqrt(ms + EPS) * nw_ref[:, cl]
            o_ref[rows, cl] = (o * _silu(g_ref[rows, cl].astype(F32))).astype(o_ref.dtype)
        return carry

    lax.fori_loop(0, n_chunks * n_groups, head_group, 0)


def _hgrn(proj, lb, norm_w, batch, seq, width, ts):
    t = proj.shape[0]
    n_heads = width // HEAD_DIM
    ts = _tile(seq, ts, HG_CHUNK)
    n_chunks = ts // HG_CHUNK
    steps = seq // ts
    row_map = lambda col: (lambda b, s: (b * steps + s, col))
    body = functools.partial(_hgrn_body, n_heads=n_heads, n_chunks=n_chunks)
    return pl.pallas_call(
        body,
        grid=(batch, steps),
        in_specs=[pl.BlockSpec((ts, width), row_map(0)),
                  pl.BlockSpec((ts, width), row_map(1)),
                  pl.BlockSpec((ts, width), row_map(2)),
                  pl.BlockSpec((ts, width), row_map(3)),
                  pl.BlockSpec((1, width), lambda b, s: (0, 0)),
                  pl.BlockSpec((1, width), lambda b, s: (0, 0))],
        out_specs=pl.BlockSpec((ts, width), row_map(0)),
        out_shape=jax.ShapeDtypeStruct((t, width), BF16),
        scratch_shapes=[pltpu.VMEM((n_heads, HEAD_DIM, HEAD_DIM), F32),
                        pltpu.VMEM((HG_CHUNK, HG_CHUNK), jnp.int32)],
        compiler_params=_cparams(("parallel", "arbitrary")),
        name="hgrn2",
    )(proj, proj, proj, proj, lb, norm_w)


def _lru_body(x_ref, y_ref, cw_ref, cb_ref, wg_ref, ba_ref, bx_ref, lam_ref, nw_ref,
              o_ref, xs_ref, h_ref, *, n_blocks, conv_w):
    tl, width = x_ref.shape
    bd = width // n_blocks
    halo = SUBLANES
    first = pl.program_id(1) == 0

    @pl.when(first)
    def _():
        xs_ref[0:halo, :] = jnp.zeros((halo, width), F32)
        h_ref[...] = jnp.zeros_like(h_ref)

    xs_ref[halo:, :] = x_ref[...].astype(F32)
    xb = cb_ref[...] + xs_ref[halo:, :] * cw_ref[conv_w - 1:conv_w, :]
    for j in range(conv_w - 1):
        d = conv_w - 1 - j
        xb = xb + xs_ref[halo - d:halo - d + tl, :] * cw_ref[j:j + 1, :]
    xs_ref[0:halo, :] = xs_ref[tl:tl + halo, :]

    xbb = xb.astype(BF16)
    pre = [jnp.dot(xbb[:, n * bd:(n + 1) * bd], wg_ref[n], preferred_element_type=F32)
           for n in range(n_blocks)]
    pre_a = jnp.concatenate([p[:, :bd] for p in pre], axis=1)
    pre_x = jnp.concatenate([p[:, bd:] for p in pre], axis=1)
    r = _sigmoid(pre_a + ba_ref[...])
    gate_i = _sigmoid(pre_x + bx_ref[...])
    lam = lam_ref[...]
    softplus = jnp.maximum(-lam, 0.0) + jnp.log(1.0 + jnp.exp(-jnp.abs(lam)))
    log_a = (-LRU_C) * r * softplus
    a = jnp.exp(log_a)
    mult = jnp.sqrt(jnp.tanh(-log_a) * (1.0 + a * a))
    row = lax.broadcasted_iota(jnp.int32, (tl, width), 0)
    mult = jnp.where(jnp.logical_and(first, row == 0), 1.0, mult)
    u = xb * gate_i * mult

    ng = tl // SUBLANES
    a3 = a.reshape(ng, SUBLANES, width)
    u3 = u.reshape(ng, SUBLANES, width)
    sub = lax.broadcasted_iota(jnp.int32, a3.shape, 1)
    d = 1
    while d < SUBLANES:
        u3 = a3 * jnp.where(sub < d, 0.0, pltpu.roll(u3, d, axis=1)) + u3
        a3 = a3 * jnp.where(sub < d, 1.0, pltpu.roll(a3, d, axis=1))
        d *= 2
    h_prev = h_ref[0:1, :]
    hs = []
    for gi in range(ng):
        hg = u3[gi] + a3[gi] * h_prev
        hs.append(hg)
        h_prev = hg[SUBLANES - 1:SUBLANES, :]
    h_ref[...] = jnp.broadcast_to(h_prev, h_ref.shape)
    h = jnp.concatenate(hs, axis=0)

    out = h * jax.nn.gelu(y_ref[...].astype(F32))
    ms = jnp.mean(out * out, axis=-1, keepdims=True)
    o_ref[...] = (out * lax.rsqrt(ms + EPS) * nw_ref[...]).astype(o_ref.dtype)


def _lru(proj, conv_w, conv_b, w_gate, ba, bx, lam, norm_w, batch, seq, width, col0, tl):
    t = proj.shape[0]
    tl = _tile(seq, tl, SUBLANES)
    steps = seq // tl
    n_blocks = w_gate.shape[0]
    row_map = lambda col: (lambda b, s: (b * steps + s, col))
    vec = lambda r: pl.BlockSpec((r, width), lambda b, s: (0, 0))
    body = functools.partial(_lru_body, n_blocks=n_blocks, conv_w=conv_w.shape[0])
    return pl.pallas_call(
        body,
        grid=(batch, steps),
        in_specs=[pl.BlockSpec((tl, width), row_map(col0)),
                  pl.BlockSpec((tl, width), row_map(col0 + 1)),
                  vec(conv_w.shape[0]), vec(1),
                  pl.BlockSpec(w_gate.shape, lambda b, s: (0, 0, 0)),
                  vec(1), vec(1), vec(1), vec(1)],
        out_specs=pl.BlockSpec((tl, width), row_map(0)),
        out_shape=jax.ShapeDtypeStruct((t, width), BF16),
        scratch_shapes=[pltpu.VMEM((SUBLANES + tl, width), F32), pltpu.VMEM((SUBLANES, width), F32)],
        compiler_params=_cparams(("parallel", "arbitrary")),
        name="rglru",
    )(proj, proj, conv_w, conv_b, w_gate, ba, bx, lam, norm_w)


def _out_proj_body(a_ref, b_ref, wa_ref, wb_ref, x_ref, o_ref):
    acc = jnp.dot(a_ref[...], wa_ref[...], preferred_element_type=F32)
    acc = acc + jnp.dot(b_ref[...], wb_ref[...], preferred_element_type=F32)
    o_ref[...] = x_ref[...] + acc


def _out_proj(o_hg, o_lru, w_out, x, tm, tn):
    m, ka = o_hg.shape
    kb = o_lru.shape[1]
    n = w_out.shape[1]
    tm, tn = _tile(m, tm, SUBLANES), _tile(n, tn, LANES)
    assert ka == kb
    return pl.pallas_call(
        _out_proj_body,
        grid=(m // tm, n // tn),
        in_specs=[pl.BlockSpec((tm, ka), lambda i, j: (i, 0)),
                  pl.BlockSpec((tm, kb), lambda i, j: (i, 0)),
                  pl.BlockSpec((ka, tn), lambda i, j: (0, j)),
                  pl.BlockSpec((kb, tn), lambda i, j: (1, j)),
                  pl.BlockSpec((tm, tn), lambda i, j: (i, j))],
        out_specs=pl.BlockSpec((tm, tn), lambda i, j: (i, j)),
        out_shape=jax.ShapeDtypeStruct((m, n), F32),
        compiler_params=_cparams(("parallel", "arbitrary")),
        name="out_proj",
    )(o_hg, o_lru, w_out, w_out, x)


def _ffn_up_body(x_ref, wg_ref, wv_ref, cg_ref, cv_ref, bg_ref, bv_ref, o_ref, ag_ref, av_ref,
                 *, rows_per_dot, conv_w):
    seq = x_ref.shape[0]
    tf = wg_ref.shape[1]
    halo = SUBLANES
    rd = rows_per_dot
    ag_ref[0:halo, :] = jnp.zeros((halo, tf), F32)
    av_ref[0:halo, :] = jnp.zeros((halo, tf), F32)

    def conv(acc_ref, r0, cw_ref, b_ref):
        y = b_ref[...] + acc_ref[halo + r0:halo + r0 + rd, :] * cw_ref[conv_w - 1:conv_w, :]
        for j in range(conv_w - 1):
            d = conv_w - 1 - j
            y = y + acc_ref[halo + r0 - d:halo + r0 - d + rd, :] * cw_ref[j:j + 1, :]
        return y

    for r0 in range(0, seq, rd):
        xs = x_ref[r0:r0 + rd, :]
        ag_ref[halo + r0:halo + r0 + rd, :] = jnp.dot(xs, wg_ref[...], preferred_element_type=F32)
        av_ref[halo + r0:halo + r0 + rd, :] = jnp.dot(xs, wv_ref[...], preferred_element_type=F32)
        gate = conv(ag_ref, r0, cg_ref, bg_ref)
        val = conv(av_ref, r0, cv_ref, bv_ref)
        o_ref[r0:r0 + rd, :] = (_silu(gate) * val).astype(o_ref.dtype)


def _ffn_up(hn, w_up, conv_w, conv_b, batch, seq, d_ff, tf, rows_per_dot):
    t, d = hn.shape
    tf = _tile(d_ff, tf, LANES)
    nf = d_ff // tf
    rows_per_dot = _tile(seq, rows_per_dot, SUBLANES)
    cw = conv_w.shape[0]
    body = functools.partial(_ffn_up_body, rows_per_dot=rows_per_dot, conv_w=cw)
    return pl.pallas_call(
        body,
        grid=(batch, nf),
        in_specs=[pl.BlockSpec((seq, d), lambda b, j: (b, 0), pipeline_mode=pl.Buffered(1)),
                  pl.BlockSpec((d, tf), lambda b, j: (0, j)),
                  pl.BlockSpec((d, tf), lambda b, j: (0, j + nf)),
                  pl.BlockSpec((cw, tf), lambda b, j: (0, j)),
                  pl.BlockSpec((cw, tf), lambda b, j: (0, j + nf)),
                  pl.BlockSpec((1, tf), lambda b, j: (0, j)),
                  pl.BlockSpec((1, tf), lambda b, j: (0, j + nf))],
        out_specs=pl.BlockSpec((seq, tf), lambda b, j: (b, j)),
        out_shape=jax.ShapeDtypeStruct((t, d_ff), BF16),
        scratch_shapes=[pltpu.VMEM((SUBLANES + seq, tf), F32), pltpu.VMEM((SUBLANES + seq, tf), F32)],
        compiler_params=_cparams(("parallel", "arbitrary")),
        name="ffn_up",
    )(hn, w_up, w_up, conv_w, conv_w, conv_b, conv_b)


def _ffn_down_body(a_ref, w_ref, h_ref, o_ref):
    o_ref[...] = h_ref[...] + jnp.dot(a_ref[...], w_ref[...], preferred_element_type=F32)


def _ffn_down(act, w_down, h, tm, tn):
    m, k = act.shape
    n = w_down.shape[1]
    tm, tn = _tile(m, tm, SUBLANES), _tile(n, tn, LANES)
    return pl.pallas_call(
        _ffn_down_body,
        grid=(n // tn, m // tm),
        in_specs=[pl.BlockSpec((tm, k), lambda j, i: (i, 0)),
                  pl.BlockSpec((k, tn), lambda j, i: (0, j), pipeline_mode=pl.Buffered(1)),
                  pl.BlockSpec((tm, tn), lambda j, i: (i, j))],
        out_specs=pl.BlockSpec((tm, tn), lambda j, i: (i, j)),
        out_shape=jax.ShapeDtypeStruct((m, n), F32),
        compiler_params=_cparams(("parallel", "arbitrary")),
        name="ffn_down",
    )(act, w_down, h)


def kernel(x, ln1_w, w_in, lb_gamma, hg_norm_w, lru_conv_w, lru_conv_b, lru_wa, lru_ba, lru_wx, lru_bx,
           lru_lambda, lru_norm_w, w_out, ln2_w, ffn_w_up, ffn_conv_w, ffn_conv_b, ffn_w_down,
           final_norm_w):
    batch, seq, d = x.shape
    depth = ln1_w.shape[0]
    hg_w = hg_norm_w.shape[1]
    lru_w = lru_lambda.shape[1]
    d_ff = ffn_w_down.shape[1]
    assert hg_w == lru_w and w_in.shape[2] == 4 * hg_w + 2 * lru_w
    t = batch * seq
    row = lambda v: v.reshape(1, -1).astype(F32)

    lb_all = jnp.cumsum(jax.nn.softmax(lb_gamma.astype(F32), axis=0), axis=0)
    h = x.reshape(t, d)
    for l in range(depth):
        hn = _rmsnorm(h, ln1_w[l], BF16, tm=512)
        proj = _matmul(hn, w_in[l].astype(BF16), BF16, tm=1024, tn=1024)
        o_hg = _hgrn(proj, row(lb_all[l]), row(hg_norm_w[l]), batch, seq, hg_w, ts=512)
        w_gate = jnp.concatenate([lru_wa[l], lru_wx[l]], axis=-1).astype(BF16)
        o_lru = _lru(proj, lru_conv_w[l].astype(F32), row(lru_conv_b[l]), w_gate, row(lru_ba[l]),
                     row(lru_bx[l]), row(lru_lambda[l]), row(lru_norm_w[l]), batch, seq, lru_w,
                     col0=4, tl=256)
        h = _out_proj(o_hg, o_lru, w_out[l].astype(BF16), h, tm=1024, tn=1024)
        hn = _rmsnorm(h, ln2_w[l], BF16, tm=512)
        act = _ffn_up(hn, ffn_w_up[l].astype(BF16), ffn_conv_w[l].astype(F32), row(ffn_conv_b[l]),
                      batch, seq, d_ff, tf=256, rows_per_dot=256)
        h = _ffn_down(act, ffn_w_down[l].astype(BF16), h, tm=256, tn=1024)
    out = _rmsnorm(h, final_norm_w, x.dtype, tm=512)
    return out.reshape(batch, seq, d)
```

```python
import functools

import jax
import jax.numpy as jnp
from jax import lax
from jax.experimental import pallas as pl
from jax.experimental.pallas import tpu as pltpu

EPS = 1e-6
LRU_C = 8.0
HEAD_DIM = 128
HG_CHUNK = 128
HG_GROUP = 4
LANES = 128
SUBLANES = 8
VMEM_LIMIT = 56 * 1024 * 1024

F32 = jnp.float32
BF16 = jnp.bfloat16


def _cparams(sem):
    return pltpu.CompilerParams(dimension_semantics=sem, vmem_limit_bytes=VMEM_LIMIT)


def _tile(dim, pref, unit):
    if dim <= pref:
        return dim
    t = (pref // unit) * unit
    while dim % t:
        t -= unit
    assert t > 0
    return t


def _neg_abs(x):
    bits = lax.bitcast_convert_type(x, jnp.int32) | jnp.int32(-2 ** 31)
    return lax.bitcast_convert_type(bits, F32)


def _silu(x):
    hx = 0.5 * x
    return hx + hx * jnp.tanh(hx)


def _sigmoid(x):
    return 0.5 + 0.5 * jnp.tanh(0.5 * x)


def _dot_nt(a, b):
    return lax.dot_general(a, b, (((1,), (1,)), ((), ())), preferred_element_type=F32)


def _dot_tn(a, b):
    return lax.dot_general(a, b, (((0,), (0,)), ((), ())), preferred_element_type=F32)


def _rmsnorm_body(x_ref, w_ref, o_ref):
    x = x_ref[...].astype(F32)
    ms = jnp.mean(x * x, axis=-1, keepdims=True)
    o_ref[...] = (x * lax.rsqrt(ms + EPS) * w_ref[...]).astype(o_ref.dtype)


def _rmsnorm(x, w, out_dtype, tm):
    m, d = x.shape
    tm = _tile(m, tm, SUBLANES)
    return pl.pallas_call(
        _rmsnorm_body,
        grid=(m // tm,),
        in_specs=[pl.BlockSpec((tm, d), lambda i: (i, 0)),
                  pl.BlockSpec((1, d), lambda i: (0, 0))],
        out_specs=pl.BlockSpec((tm, d), lambda i: (i, 0)),
        out_shape=jax.ShapeDtypeStruct((m, d), out_dtype),
        compiler_params=_cparams(("parallel",)),
        name="rmsnorm",
    )(x, w.reshape(1, d).astype(F32))


def _matmul_body(x_ref, w_ref, *refs):
    n_cast = (len(refs) - 1) // 2
    o_ref = refs[n_cast]
    o_ref[...] = jnp.dot(x_ref[...], w_ref[...], preferred_element_type=F32).astype(o_ref.dtype)
    for src, dst in zip(refs[:n_cast], refs[n_cast + 1:]):
        dst[...] = src[...].astype(dst.dtype)


def _cast_spec(c, nj, steps):
    rb = next(r for r in range(16, c.shape[0] + 1, 16) if c.shape[0] % r == 0 and c.shape[0] // r <= steps)
    nb = c.shape[0] // rb
    return pl.BlockSpec((rb, c.shape[1]), lambda i, j: (jnp.minimum(i * nj + j, nb - 1), 0))


def _matmul(x, w, out_dtype, tm, tn, casts=()):
    m, k = x.shape
    n = w.shape[1]
    tm, tn = _tile(m, tm, SUBLANES), _tile(n, tn, LANES)
    ni, nj = m // tm, n // tn
    cast_specs = [_cast_spec(c, nj, ni * nj) for c in casts]
    res = pl.pallas_call(
        _matmul_body,
        grid=(ni, nj),
        in_specs=[pl.BlockSpec((tm, k), lambda i, j: (i, 0)),
                  pl.BlockSpec((k, tn), lambda i, j: (0, j))] + cast_specs,
        out_specs=[pl.BlockSpec((tm, tn), lambda i, j: (i, j))] + cast_specs,
        out_shape=[jax.ShapeDtypeStruct((m, n), out_dtype)] + [jax.ShapeDtypeStruct(c.shape, BF16) for c in casts],
        compiler_params=_cparams(("arbitrary", "arbitrary")),
        name="in_proj",
    )(x, w, *casts)
    return res[0], res[1:]


def _hgrn_levels(c):
    halves = []
    h = c // 2
    while h >= 1:
        halves.append(h)
        h //= 2
    return halves


def _boundary_rows_in_vreg(b, half):
    c, d = b.shape
    blk = 2 * half
    b3 = b.reshape(c // SUBLANES, SUBLANES, d)
    row = lax.broadcasted_iota(jnp.int32, b3.shape, 1)
    out = None
    for r0 in range(0, SUBLANES, blk):
        p = r0 + half - 1
        cand = jnp.broadcast_to(b3[:, p:p + 1, :], b3.shape)
        out = cand if out is None else jnp.where(row >= r0, cand, out)
    return out.reshape(c, d)


def _hgrn_body(q_ref, f_ref, v_ref, g_ref, lb_ref, nw_ref, o_ref, st_ref, lvl_ref, *, n_heads, n_chunks):
    c = HG_CHUNK
    dk = HEAD_DIM
    halves = _hgrn_levels(c)
    n_lv = len(halves)
    nv = c // SUBLANES
    group = HG_GROUP if n_heads % HG_GROUP == 0 else 1
    n_groups = n_heads // group

    @pl.when(pl.program_id(1) == 0)
    def _():
        st_ref[...] = jnp.zeros_like(st_ref)

    ti = lax.broadcasted_iota(jnp.int32, (c, c), 0)
    si = lax.broadcasted_iota(jnp.int32, (c, c), 1)
    x = ti ^ si
    lvl = jnp.where(x == 0, -1, 0)
    for j in range(1, n_lv):
        lvl = lvl + (x >= (1 << j)).astype(jnp.int32)
    lvl_ref[...] = jnp.where(si > ti, -2, lvl)
    tril2 = jnp.concatenate([(si <= ti).astype(BF16)] * 2, axis=1)
    odd = (lax.broadcasted_iota(jnp.int32, (c, dk), 0) & 1) == 1

    def head_group(idx, carry):
        ci = idx // n_groups
        gi = idx % n_groups
        rows = pl.ds(pl.multiple_of(ci * c, c), c)
        heads = [gi * group + j for j in range(group)]
        cols = [pl.ds(pl.multiple_of(h * dk, dk), dk) for h in heads]

        qs, fs, ks, vs, b2s, kbs = [], [], [], [], [], []
        for cl in cols:
            lb = lb_ref[:, cl]
            q = _silu(q_ref[rows, cl].astype(F32))
            f = lb + (1.0 - lb) * _sigmoid(f_ref[rows, cl].astype(F32))
            k = 1.0 - f
            g2 = jnp.log2(f)
            g_hi = g2.astype(BF16)
            g_lo = (g2 - g_hi.astype(F32)).astype(BF16)
            b2s.append(jnp.dot(tril2, jnp.concatenate([g_hi, g_lo], axis=0), preferred_element_type=F32))
            qs.append(q)
            fs.append(f)
            ks.append(k)
            kbs.append(k.astype(BF16))
            vs.append(v_ref[rows, cl])

        scs = []
        for q, f, kb in zip(qs, fs, kbs):
            s_d = _dot_nt(q.astype(BF16), kb)
            s_1 = _dot_nt((q * jnp.where(odd, f, 1.0)).astype(BF16), kb)
            sc = []
            for r in range(nv):
                rs = slice(r * SUBLANES, (r + 1) * SUBLANES)
                lv = lvl_ref[rs, :]
                sc.append(jnp.where(lv == 0, s_1[rs, :], jnp.where(lv == -1, s_d[rs, :], 0.0)))
            scs.append(sc)

        inters = []
        for h, q, k, v, b2 in zip(heads, qs, ks, vs, b2s):
            b_last = b2[c - 1:c, :]
            st = st_ref[h]
            inters.append(_dot_nt((q * jnp.exp2(b2)).astype(BF16), st.astype(BF16)))
            k_end = (k * jnp.exp2(b_last - b2)).astype(BF16)
            st_ref[h] = jnp.exp2(b_last) * st + _dot_tn(v, k_end)

        for li, half in enumerate(halves[:-1]):
            level = n_lv - 1 - li
            blk = 2 * half
            prods = []
            for q, k, b2 in zip(qs, ks, b2s):
                if half >= SUBLANES:
                    lhs, rhs = [], []
                    for r0 in range(0, c, blk):
                        bp = b2[r0 + half - 1:r0 + half, :]
                        lo = slice(r0, r0 + half)
                        up = slice(r0 + half, r0 + blk)
                        lhs.append(q[up, :] * jnp.exp2(b2[up, :] - bp))
                        rhs.append(k[lo, :] * jnp.exp2(bp - b2[lo, :]))
                        rhs.append(k[up, :])
                    lhs = lhs[0] if len(lhs) == 1 else jnp.concatenate(lhs, axis=0)
                    prods.append(_dot_nt(lhs.astype(BF16), jnp.concatenate(rhs, axis=0).astype(BF16)))
                else:
                    e = jnp.exp2(_neg_abs(b2 - _boundary_rows_in_vreg(b2, half)))
                    prods.append(_dot_nt((q * e).astype(BF16), (k * e).astype(BF16)))
            if half >= SUBLANES:
                row_of = [r // SUBLANES for r0 in range(0, c, blk) for r in range(r0 + half, r0 + blk, SUBLANES)]
            else:
                row_of = list(range(nv))
            for sc, s_l in zip(scs, prods):
                for i, r in enumerate(row_of):
                    rs = slice(r * SUBLANES, (r + 1) * SUBLANES)
                    piece = s_l[i * SUBLANES:(i + 1) * SUBLANES, :]
                    sc[r] = jnp.where(lvl_ref[rs, :] == level, piece, sc[r])

        outs = []
        for sc, v, inter in zip(scs, vs, inters):
            scores = jnp.concatenate(sc, axis=0).astype(BF16)
            outs.append(inter + jnp.dot(scores, v, preferred_element_type=F32))
        for o, cl in zip(outs, cols):
            ms = jnp.mean(o * o, axis=-1, keepdims=True)
            o = o * lax.rsqrt(ms + EPS) * nw_ref[:, cl]
            o_ref[rows, cl] = (o * _silu(g_ref[rows, cl].astype(F32))).astype(o_ref.dtype)
        return carry

    lax.fori_loop(0, n_chunks * n_groups, head_group, 0)


def _hgrn(proj, lb, norm_w, batch, seq, width, ts):
    t = proj.shape[0]
    n_heads = width // HEAD_DIM
    ts = _tile(seq, ts, HG_CHUNK)
    n_chunks = ts // HG_CHUNK
    steps = seq // ts
    row_map = lambda col: (lambda b, s: (b * steps + s, col))
    body = functools.partial(_hgrn_body, n_heads=n_heads, n_chunks=n_chunks)
    return pl.pallas_call(
        body,
        grid=(batch, steps),
        in_specs=[pl.BlockSpec((ts, width), row_map(0)),
                  pl.BlockSpec((ts, width), row_map(1)),
                  pl.BlockSpec((ts, width), row_map(2)),
                  pl.BlockSpec((ts, width), row_map(3)),
                  pl.BlockSpec((1, width), lambda b, s: (0, 0)),
                  pl.BlockSpec((1, width), lambda b, s: (0, 0))],
        out_specs=pl.BlockSpec((ts, width), row_map(0)),
        out_shape=jax.ShapeDtypeStruct((t, width), BF16),
        scratch_shapes=[pltpu.VMEM((n_heads, HEAD_DIM, HEAD_DIM), F32),
                        pltpu.VMEM((HG_CHUNK, HG_CHUNK), jnp.int32)],
        compiler_params=_cparams(("parallel", "arbitrary")),
        name="hgrn2",
    )(proj, proj, proj, proj, lb, norm_w)


def _lru_body(x_ref, y_ref, cw_ref, cb_ref, wg_ref, ba_ref, bx_ref, lam_ref, nw_ref,
              o_ref, xs_ref, h_ref, *, n_blocks, conv_w):
    tl, width = x_ref.shape
    bd = width // n_blocks
    halo = SUBLANES
    first = pl.program_id(1) == 0

    @pl.when(first)
    def _():
        xs_ref[0:halo, :] = jnp.zeros((halo, width), F32)
        h_ref[...] = jnp.zeros_like(h_ref)

    xs_ref[halo:, :] = x_ref[...].astype(F32)
    xb = cb_ref[...] + xs_ref[halo:, :] * cw_ref[conv_w - 1:conv_w, :]
    for j in range(conv_w - 1):
        d = conv_w - 1 - j
        xb = xb + xs_ref[halo - d:halo - d + tl, :] * cw_ref[j:j + 1, :]
    xs_ref[0:halo, :] = xs_ref[tl:tl + halo, :]

    xbb = xb.astype(BF16)
    pre = [jnp.dot(xbb[:, n * bd:(n + 1) * bd], wg_ref[n], preferred_element_type=F32)
           for n in range(n_blocks)]
    pre_a = jnp.concatenate([p[:, :bd] for p in pre], axis=1)
    pre_x = jnp.concatenate([p[:, bd:] for p in pre], axis=1)
    r = _sigmoid(pre_a + ba_ref[...])
    gate_i = _sigmoid(pre_x + bx_ref[...])
    lam = lam_ref[...]
    softplus = jnp.maximum(-lam, 0.0) + jnp.log(1.0 + jnp.exp(-jnp.abs(lam)))
    log_a = (-LRU_C) * r * softplus
    a = jnp.exp(log_a)
    mult = jnp.sqrt(jnp.tanh(-log_a) * (1.0 + a * a))
    row = lax.broadcasted_iota(jnp.int32, (tl, width), 0)
    mult = jnp.where(jnp.logical_and(first, row == 0), 1.0, mult)
    u = xb * gate_i * mult

    ng = tl // SUBLANES
    a3 = a.reshape(ng, SUBLANES, width)
    u3 = u.reshape(ng, SUBLANES, width)
    sub = lax.broadcasted_iota(jnp.int32, a3.shape, 1)
    d = 1
    while d < SUBLANES:
        u3 = a3 * jnp.where(sub < d, 0.0, pltpu.roll(u3, d, axis=1)) + u3
        a3 = a3 * jnp.where(sub < d, 1.0, pltpu.roll(a3, d, axis=1))
        d *= 2
    h_prev = h_ref[0:1, :]
    hs = []
    for gi in range(ng):
        hg = u3[gi] + a3[gi] * h_prev
        hs.append(hg)
        h_prev = hg[SUBLANES - 1:SUBLANES, :]
    h_ref[...] = jnp.broadcast_to(h_prev, h_ref.shape)
    h = jnp.concatenate(hs, axis=0)

    out = h * jax.nn.gelu(y_ref[...].astype(F32))
    ms = jnp.mean(out * out, axis=-1, keepdims=True)
    o_ref[...] = (out * lax.rsqrt(ms + EPS) * nw_ref[...]).astype(o_ref.dtype)


def _lru(proj, conv_w, conv_b, w_gate, ba, bx, lam, norm_w, batch, seq, width, col0, tl):
    t = proj.shape[0]
    tl = _tile(seq, tl, SUBLANES)
    steps = seq // tl
    n_blocks = w_gate.shape[0]
    row_map = lambda col: (lambda b, s: (b * steps + s, col))
    vec = lambda r: pl.BlockSpec((r, width), lambda b, s: (0, 0))
    body = functools.partial(_lru_body, n_blocks=n_blocks, conv_w=conv_w.shape[0])
    return pl.pallas_call(
        body,
        grid=(batch, steps),
        in_specs=[pl.BlockSpec((tl, width), row_map(col0)),
                  pl.BlockSpec((tl, width), row_map(col0 + 1)),
                  vec(conv_w.shape[0]), vec(1),
                  pl.BlockSpec(w_gate.shape, lambda b, s: (0, 0, 0)),
                  vec(1), vec(1), vec(1), vec(1)],
        out_specs=pl.BlockSpec((tl, width), row_map(0)),
        out_shape=jax.ShapeDtypeStruct((t, width), BF16),
        scratch_shapes=[pltpu.VMEM((SUBLANES + tl, width), F32), pltpu.VMEM((SUBLANES, width), F32)],
        compiler_params=_cparams(("parallel", "arbitrary")),
        name="rglru",
    )(proj, proj, conv_w, conv_b, w_gate, ba, bx, lam, norm_w)


def _out_proj_body(a_ref, b_ref, wa_ref, wb_ref, x_ref, o_ref):
    acc = jnp.dot(a_ref[...], wa_ref[...], preferred_element_type=F32)
    acc = acc + jnp.dot(b_ref[...], wb_ref[...], preferred_element_type=F32)
    o_ref[...] = x_ref[...] + acc


def _out_proj(o_hg, o_lru, w_out, x, tm, tn):
    m, ka = o_hg.shape
    kb = o_lru.shape[1]
    n = w_out.shape[1]
    tm, tn = _tile(m, tm, SUBLANES), _tile(n, tn, LANES)
    assert ka == kb
    return pl.pallas_call(
        _out_proj_body,
        grid=(m // tm, n // tn),
        in_specs=[pl.BlockSpec((tm, ka), lambda i, j: (i, 0)),
                  pl.BlockSpec((tm, kb), lambda i, j: (i, 0)),
                  pl.BlockSpec((ka, tn), lambda i, j: (0, j)),
                  pl.BlockSpec((kb, tn), lambda i, j: (1, j)),
                  pl.BlockSpec((tm, tn), lambda i, j: (i, j))],
        out_specs=pl.BlockSpec((tm, tn), lambda i, j: (i, j)),
        out_shape=jax.ShapeDtypeStruct((m, n), F32),
        compiler_params=_cparams(("parallel", "arbitrary")),
        name="out_proj",
    )(o_hg, o_lru, w_out, w_out, x)


def _ffn_up_body(x_ref, wg_ref, wv_ref, cg_ref, cv_ref, bg_ref, bv_ref, wd_ref, o_ref, wdb_ref,
                 ag_ref, av_ref, xs_ref, *, rows_per_dot, conv_w):
    seq = x_ref.shape[0]
    tf = wg_ref.shape[1]
    halo = SUBLANES
    rd = rows_per_dot
    wdb_ref[...] = wd_ref[...].astype(wdb_ref.dtype)

    @pl.when(pl.program_id(1) == 0)
    def _():
        xs_ref[...] = x_ref[...]

    ag_ref[0:halo, :] = jnp.zeros((halo, tf), F32)
    av_ref[0:halo, :] = jnp.zeros((halo, tf), F32)

    def conv(acc_ref, r0, cw_ref, b_ref):
        y = b_ref[...] + acc_ref[halo + r0:halo + r0 + rd, :] * cw_ref[conv_w - 1:conv_w, :]
        for j in range(conv_w - 1):
            d = conv_w - 1 - j
            y = y + acc_ref[halo + r0 - d:halo + r0 - d + rd, :] * cw_ref[j:j + 1, :]
        return y

    for r0 in range(0, seq, rd):
        xs = xs_ref[r0:r0 + rd, :]
        ag_ref[halo + r0:halo + r0 + rd, :] = jnp.dot(xs, wg_ref[...], preferred_element_type=F32)
        av_ref[halo + r0:halo + r0 + rd, :] = jnp.dot(xs, wv_ref[...], preferred_element_type=F32)
        gate = conv(ag_ref, r0, cg_ref, bg_ref)
        val = conv(av_ref, r0, cv_ref, bv_ref)
        o_ref[r0:r0 + rd, :] = (_silu(gate) * val).astype(o_ref.dtype)


def _ffn_up(hn, w_up, conv_w, conv_b, w_down, batch, seq, d_ff, tf, rows_per_dot):
    t, d = hn.shape
    tf = _tile(d_ff, tf, LANES)
    nf = d_ff // tf
    rows_per_dot = _tile(seq, rows_per_dot, SUBLANES)
    cw = conv_w.shape[0]
    body = functools.partial(_ffn_up_body, rows_per_dot=rows_per_dot, conv_w=cw)
    cast_spec = _cast_spec(w_down, nf, batch * nf)
    return pl.pallas_call(
        body,
        grid=(batch, nf),
        in_specs=[pl.BlockSpec((seq, d), lambda b, j: (b, 0), pipeline_mode=pl.Buffered(1)),
                  pl.BlockSpec((d, tf), lambda b, j: (0, j)),
                  pl.BlockSpec((d, tf), lambda b, j: (0, j + nf)),
                  pl.BlockSpec((cw, tf), lambda b, j: (0, j)),
                  pl.BlockSpec((cw, tf), lambda b, j: (0, j + nf)),
                  pl.BlockSpec((1, tf), lambda b, j: (0, j)),
                  pl.BlockSpec((1, tf), lambda b, j: (0, j + nf)),
                  cast_spec],
        out_specs=[pl.BlockSpec((seq, tf), lambda b, j: (b, j)), cast_spec],
        out_shape=[jax.ShapeDtypeStruct((t, d_ff), BF16), jax.ShapeDtypeStruct(w_down.shape, BF16)],
        scratch_shapes=[pltpu.VMEM((SUBLANES + seq, tf), F32), pltpu.VMEM((SUBLANES + seq, tf), F32),
                        pltpu.VMEM((seq, d), BF16)],
        compiler_params=_cparams(("arbitrary", "arbitrary")),
        name="ffn_up",
    )(hn, w_up, w_up, conv_w, conv_w, conv_b, conv_b, w_down)


def _ffn_down_body(a_ref, w_ref, h_ref, o_ref):
    o_ref[...] = h_ref[...] + jnp.dot(a_ref[...], w_ref[...], preferred_element_type=F32)


def _ffn_down(act, w_down, h, tm, tn):
    m, k = act.shape
    n = w_down.shape[1]
    tm, tn = _tile(m, tm, SUBLANES), _tile(n, tn, LANES)
    return pl.pallas_call(
        _ffn_down_body,
        grid=(n // tn, m // tm),
        in_specs=[pl.BlockSpec((tm, k), lambda j, i: (i, 0)),
                  pl.BlockSpec((k, tn), lambda j, i: (0, j), pipeline_mode=pl.Buffered(1)),
                  pl.BlockSpec((tm, tn), lambda j, i: (i, j))],
        out_specs=pl.BlockSpec((tm, tn), lambda j, i: (i, j)),
        out_shape=jax.ShapeDtypeStruct((m, n), F32),
        compiler_params=_cparams(("parallel", "arbitrary")),
        name="ffn_down",
    )(act, w_down, h)


def kernel(x, ln1_w, w_in, lb_gamma, hg_norm_w, lru_conv_w, lru_conv_b, lru_wa, lru_ba, lru_wx, lru_bx,
           lru_lambda, lru_norm_w, w_out, ln2_w, ffn_w_up, ffn_conv_w, ffn_conv_b, ffn_w_down,
           final_norm_w):
    batch, seq, d = x.shape
    depth = ln1_w.shape[0]
    hg_w = hg_norm_w.shape[1]
    lru_w = lru_lambda.shape[1]
    d_ff = ffn_w_down.shape[1]
    assert hg_w == lru_w and w_in.shape[2] == 4 * hg_w + 2 * lru_w
    t = batch * seq
    row = lambda v: v.reshape(1, -1).astype(F32)

    lb_all = jnp.cumsum(jax.nn.softmax(lb_gamma.astype(F32), axis=0), axis=0)
    h = x.reshape(t, d)
    for l in range(depth):
        hn = _rmsnorm(h, ln1_w[l], BF16, tm=512)
        proj, (w_out_b, w_up_b) = _matmul(hn, w_in[l].astype(BF16), BF16, tm=1024, tn=1024,
                                          casts=(w_out[l].astype(F32), ffn_w_up[l].astype(F32)))
        o_hg = _hgrn(proj, row(lb_all[l]), row(hg_norm_w[l]), batch, seq, hg_w, ts=512)
        w_gate = jnp.concatenate([lru_wa[l], lru_wx[l]], axis=-1).astype(BF16)
        o_lru = _lru(proj, lru_conv_w[l].astype(F32), row(lru_conv_b[l]), w_gate, row(lru_ba[l]),
                     row(lru_bx[l]), row(lru_lambda[l]), row(lru_norm_w[l]), batch, seq, lru_w,
                     col0=4, tl=256)
        h = _out_proj(o_hg, o_lru, w_out_b, h, tm=1024, tn=1024)
        hn = _rmsnorm(h, ln2_w[l], BF16, tm=512)
        act, w_down_b = _ffn_up(hn, w_up_b, ffn_conv_w[l].astype(F32), row(ffn_conv_b[l]),
                                ffn_w_down[l].astype(F32), batch, seq, d_ff, tf=256, rows_per_dot=256)
        h = _ffn_down(act, w_down_b, h, tm=512, tn=1024)
    out = _rmsnorm(h, final_norm_w, x.dtype, tm=512)
    return out.reshape(batch, seq, d)
```

```python
import functools

import jax
import jax.numpy as jnp
from jax import lax
from jax.experimental import pallas as pl
from jax.experimental.pallas import tpu as pltpu

EPS = 1e-6
LRU_C = 8.0
HEAD_DIM = 128
HG_CHUNK = 128
HG_GROUP = 8
LANES = 128
SUBLANES = 8
VMEM_LIMIT = 56 * 1024 * 1024

F32 = jnp.float32
BF16 = jnp.bfloat16


def _cparams(sem):
    return pltpu.CompilerParams(dimension_semantics=sem, vmem_limit_bytes=VMEM_LIMIT)


def _tile(dim, pref, unit):
    if dim <= pref:
        return dim
    t = (pref // unit) * unit
    while dim % t:
        t -= unit
    assert t > 0
    return t


def _neg_abs(x):
    bits = lax.bitcast_convert_type(x, jnp.int32) | jnp.int32(-2 ** 31)
    return lax.bitcast_convert_type(bits, F32)


def _silu_of_half(hx):
    return hx + hx * jnp.tanh(hx)


def _sigmoid_of_half(hx):
    return 0.5 + 0.5 * jnp.tanh(hx)


def _dot_nt(a, b):
    return lax.dot_general(a, b, (((1,), (1,)), ((), ())), preferred_element_type=F32)


def _dot_tn(a, b):
    return lax.dot_general(a, b, (((0,), (0,)), ((), ())), preferred_element_type=F32)


def _rmsnorm_body(x_ref, w_ref, o_ref):
    x = x_ref[...].astype(F32)
    ms = jnp.mean(x * x, axis=-1, keepdims=True)
    o_ref[...] = (x * lax.rsqrt(ms + EPS) * w_ref[...]).astype(o_ref.dtype)


def _rmsnorm(x, w, out_dtype, tm):
    m, d = x.shape
    tm = _tile(m, tm, SUBLANES)
    return pl.pallas_call(
        _rmsnorm_body,
        grid=(m // tm,),
        in_specs=[pl.BlockSpec((tm, d), lambda i: (i, 0)),
                  pl.BlockSpec((1, d), lambda i: (0, 0))],
        out_specs=pl.BlockSpec((tm, d), lambda i: (i, 0)),
        out_shape=jax.ShapeDtypeStruct((m, d), out_dtype),
        compiler_params=_cparams(("parallel",)),
        name="rmsnorm",
    )(x, w.reshape(1, d).astype(F32))


def _matmul_body(x_ref, w_ref, *refs):
    n_cast = (len(refs) - 1) // 2
    o_ref = refs[n_cast]
    o_ref[...] = jnp.dot(x_ref[...], w_ref[...], preferred_element_type=F32).astype(o_ref.dtype)
    for src, dst in zip(refs[:n_cast], refs[n_cast + 1:]):
        if len(dst.shape) == 2:
            dst[...] = src[...].astype(dst.dtype)
        else:
            ct = dst.shape[2]
            for c in range(dst.shape[0]):
                dst[c] = src[:, c * ct:(c + 1) * ct].astype(dst.dtype)


def _cast_spec(c, nj, steps, col_tile=None):
    rows, cols = c.shape
    rb = next(r for r in range(16, rows + 1, 16) if rows % r == 0 and rows // r <= steps)
    nb = rows // rb
    blk = lambda i, j: jnp.minimum(i * nj + j, nb - 1)
    src = pl.BlockSpec((rb, cols), lambda i, j: (blk(i, j), 0))
    if col_tile is None:
        return src, src, jax.ShapeDtypeStruct((rows, cols), BF16)
    nct = cols // col_tile
    dst = pl.BlockSpec((nct, rb, col_tile), lambda i, j: (0, blk(i, j), 0))
    return src, dst, jax.ShapeDtypeStruct((nct, rows, col_tile), BF16)


def _matmul(x, w, out_dtype, tm, tn, casts=()):
    m, k = x.shape
    n = w.shape[1]
    tm, tn = _tile(m, tm, SUBLANES), _tile(n, tn, LANES)
    ni, nj = m // tm, n // tn
    specs = [_cast_spec(c, nj, ni * nj, ct) for c, ct in casts]
    res = pl.pallas_call(
        _matmul_body,
        grid=(ni, nj),
        in_specs=[pl.BlockSpec((tm, k), lambda i, j: (i, 0)),
                  pl.BlockSpec((k, tn), lambda i, j: (0, j))] + [s[0] for s in specs],
        out_specs=[pl.BlockSpec((tm, tn), lambda i, j: (i, j))] + [s[1] for s in specs],
        out_shape=[jax.ShapeDtypeStruct((m, n), out_dtype)] + [s[2] for s in specs],
        compiler_params=_cparams(("arbitrary", "arbitrary")),
        name="in_proj",
    )(x, w, *[c for c, _ in casts])
    return res[0], res[1:]


def _hgrn_levels(c):
    halves = []
    h = c // 2
    while h >= 1:
        halves.append(h)
        h //= 2
    return halves


def _boundary_rows_in_vreg(b, half):
    c, d = b.shape
    blk = 2 * half
    b3 = b.reshape(c // SUBLANES, SUBLANES, d)
    row = lax.broadcasted_iota(jnp.int32, b3.shape, 1)
    out = None
    for r0 in range(0, SUBLANES, blk):
        p = r0 + half - 1
        cand = jnp.broadcast_to(b3[:, p:p + 1, :], b3.shape)
        out = cand if out is None else jnp.where(row >= r0, cand, out)
    return out.reshape(c, d)


def _hgrn_body(q_ref, f_ref, v_ref, g_ref, lb_ref, nw_ref, o_ref, st_ref, lvl_ref, *, n_heads, n_chunks):
    c = HG_CHUNK
    dk = HEAD_DIM
    halves = _hgrn_levels(c)
    n_lv = len(halves)
    nv = c // SUBLANES
    group = HG_GROUP if n_heads % HG_GROUP == 0 else 1
    n_groups = n_heads // group

    @pl.when(pl.program_id(1) == 0)
    def _():
        st_ref[...] = jnp.zeros_like(st_ref)

    ti = lax.broadcasted_iota(jnp.int32, (c, c), 0)
    si = lax.broadcasted_iota(jnp.int32, (c, c), 1)
    x = ti ^ si
    lvl = jnp.where(x == 0, -1, 0)
    for j in range(1, n_lv):
        lvl = lvl + (x >= (1 << j)).astype(jnp.int32)
    lvl_ref[...] = jnp.where(si > ti, -2, lvl)
    tril2 = jnp.concatenate([(si <= ti).astype(BF16)] * 2, axis=1)
    odd = (lax.broadcasted_iota(jnp.int32, (c, dk), 0) & 1) == 1

    def head_group(idx, carry):
        ci = idx // n_groups
        gi = idx % n_groups
        rows = pl.ds(pl.multiple_of(ci * c, c), c)
        heads = [gi * group + j for j in range(group)]
        cols = [pl.ds(pl.multiple_of(h * dk, dk), dk) for h in heads]

        qs, fs, ks, vs, b2s, kbs = [], [], [], [], [], []
        for cl in cols:
            lb = lb_ref[:, cl]
            q = _silu_of_half(q_ref[rows, cl].astype(F32))
            f = (0.5 + 0.5 * lb) + (0.5 - 0.5 * lb) * jnp.tanh(f_ref[rows, cl].astype(F32))
            k = 1.0 - f
            g2 = jnp.log2(f)
            g_hi = g2.astype(BF16)
            g_lo = (g2 - g_hi.astype(F32)).astype(BF16)
            b2s.append(jnp.dot(tril2, jnp.concatenate([g_hi, g_lo], axis=0), preferred_element_type=F32))
            qs.append(q)
            fs.append(f)
            ks.append(k)
            kbs.append(k.astype(BF16))
            vs.append(v_ref[rows, cl])

        scs = []
        for q, f, kb in zip(qs, fs, kbs):
            s_d = _dot_nt(q.astype(BF16), kb)
            s_1 = _dot_nt((q * jnp.where(odd, f, 1.0)).astype(BF16), kb)
            sc = []
            for r in range(nv):
                rs = slice(r * SUBLANES, (r + 1) * SUBLANES)
                lv = lvl_ref[rs, :]
                sc.append(jnp.where(lv == 0, s_1[rs, :], jnp.where(lv == -1, s_d[rs, :], 0.0)))
            scs.append(sc)

        inters = []
        for h, q, k, v, b2 in zip(heads, qs, ks, vs, b2s):
            b_last = b2[c - 1:c, :]
            st = st_ref[h]
            inters.append(_dot_nt((q * jnp.exp2(b2)).astype(BF16), st.astype(BF16)))
            k_end = (k * jnp.exp2(b_last - b2)).astype(BF16)
            st_ref[h] = jnp.exp2(b_last) * st + _dot_tn(v, k_end)

        for li, half in enumerate(halves[:-1]):
            level = n_lv - 1 - li
            blk = 2 * half
            prods = []
            for q, k, b2 in zip(qs, ks, b2s):
                if half >= SUBLANES:
                    lhs, rhs = [], []
                    for r0 in range(0, c, blk):
                        bp = b2[r0 + half - 1:r0 + half, :]
                        lo = slice(r0, r0 + half)
                        up = slice(r0 + half, r0 + blk)
                        lhs.append(q[up, :] * jnp.exp2(b2[up, :] - bp))
                        rhs.append(k[lo, :] * jnp.exp2(bp - b2[lo, :]))
                        rhs.append(k[up, :])
                    lhs = lhs[0] if len(lhs) == 1 else jnp.concatenate(lhs, axis=0)
                    prods.append(_dot_nt(lhs.astype(BF16), jnp.concatenate(rhs, axis=0).astype(BF16)))
                else:
                    e = jnp.exp2(_neg_abs(b2 - _boundary_rows_in_vreg(b2, half)))
                    prods.append(_dot_nt((q * e).astype(BF16), (k * e).astype(BF16)))
            if half >= SUBLANES:
                row_of = [r // SUBLANES for r0 in range(0, c, blk) for r in range(r0 + half, r0 + blk, SUBLANES)]
            else:
                row_of = list(range(nv))
            for sc, s_l in zip(scs, prods):
                for i, r in enumerate(row_of):
                    rs = slice(r * SUBLANES, (r + 1) * SUBLANES)
                    piece = s_l[i * SUBLANES:(i + 1) * SUBLANES, :]
                    sc[r] = jnp.where(lvl_ref[rs, :] == level, piece, sc[r])

        outs = []
        for sc, v, inter in zip(scs, vs, inters):
            scores = jnp.concatenate(sc, axis=0).astype(BF16)
            outs.append(inter + jnp.dot(scores, v, preferred_element_type=F32))
        for o, cl in zip(outs, cols):
            ms = jnp.mean(o * o, axis=-1, keepdims=True)
            o = o * lax.rsqrt(ms + EPS) * nw_ref[:, cl]
            o_ref[rows, cl] = (o * _silu_of_half(g_ref[rows, cl].astype(F32))).astype(o_ref.dtype)
        return carry

    lax.fori_loop(0, n_chunks * n_groups, head_group, 0)


def _hgrn(proj, lb, norm_w, batch, seq, width, ts):
    t = proj.shape[0]
    n_heads = width // HEAD_DIM
    ts = _tile(seq, ts, HG_CHUNK)
    n_chunks = ts // HG_CHUNK
    steps = seq // ts
    row_map = lambda col: (lambda b, s: (b * steps + s, col))
    body = functools.partial(_hgrn_body, n_heads=n_heads, n_chunks=n_chunks)
    return pl.pallas_call(
        body,
        grid=(batch, steps),
        in_specs=[pl.BlockSpec((ts, width), row_map(0)),
                  pl.BlockSpec((ts, width), row_map(1)),
                  pl.BlockSpec((ts, width), row_map(2)),
                  pl.BlockSpec((ts, width), row_map(3)),
                  pl.BlockSpec((1, width), lambda b, s: (0, 0)),
                  pl.BlockSpec((1, width), lambda b, s: (0, 0))],
        out_specs=pl.BlockSpec((ts, width), row_map(0)),
        out_shape=jax.ShapeDtypeStruct((t, width), BF16),
        scratch_shapes=[pltpu.VMEM((n_heads, HEAD_DIM, HEAD_DIM), F32),
                        pltpu.VMEM((HG_CHUNK, HG_CHUNK), jnp.int32)],
        compiler_params=_cparams(("parallel", "arbitrary")),
        name="hgrn2",
    )(proj, proj, proj, proj, lb, norm_w)


def _lru_body(x_ref, y_ref, cw_ref, cb_ref, wg_ref, ba_ref, bx_ref, lam_ref, nw_ref,
              o_ref, xs_ref, h_ref, *, n_blocks, conv_w):
    tl, width = x_ref.shape
    bd = width // n_blocks
    halo = SUBLANES
    first = pl.program_id(1) == 0

    @pl.when(first)
    def _():
        xs_ref[0:halo, :] = jnp.zeros((halo, width), F32)
        h_ref[...] = jnp.zeros_like(h_ref)

    xs_ref[halo:, :] = x_ref[...].astype(F32)
    xb = cb_ref[...] + xs_ref[halo:, :] * cw_ref[conv_w - 1:conv_w, :]
    for j in range(conv_w - 1):
        d = conv_w - 1 - j
        xb = xb + xs_ref[halo - d:halo - d + tl, :] * cw_ref[j:j + 1, :]
    xs_ref[0:halo, :] = xs_ref[tl:tl + halo, :]

    xbb = xb.astype(BF16)
    pre = [jnp.dot(xbb[:, n * bd:(n + 1) * bd], wg_ref[n], preferred_element_type=F32)
           for n in range(n_blocks)]
    pre_a = jnp.concatenate([p[:, :bd] for p in pre], axis=1)
    pre_x = jnp.concatenate([p[:, bd:] for p in pre], axis=1)
    r = _sigmoid_of_half(pre_a + ba_ref[...])
    gate_i = _sigmoid_of_half(pre_x + bx_ref[...])
    lam = lam_ref[...]
    softplus = jnp.maximum(-lam, 0.0) + jnp.log(1.0 + jnp.exp(-jnp.abs(lam)))
    log_a = (-LRU_C) * r * softplus
    a = jnp.exp(log_a)
    mult = jnp.sqrt(jnp.tanh(-log_a) * (1.0 + a * a))
    row = lax.broadcasted_iota(jnp.int32, (tl, width), 0)
    mult = jnp.where(jnp.logical_and(first, row == 0), 1.0, mult)
    u = xb * gate_i * mult

    ng = tl // SUBLANES
    a3 = a.reshape(ng, SUBLANES, width)
    u3 = u.reshape(ng, SUBLANES, width)
    sub = lax.broadcasted_iota(jnp.int32, a3.shape, 1)
    d = 1
    while d < SUBLANES:
        u3 = a3 * jnp.where(sub < d, 0.0, pltpu.roll(u3, d, axis=1)) + u3
        a3 = a3 * jnp.where(sub < d, 1.0, pltpu.roll(a3, d, axis=1))
        d *= 2
    h_prev = h_ref[0:1, :]
    hs = []
    for gi in range(ng):
        hg = u3[gi] + a3[gi] * h_prev
        hs.append(hg)
        h_prev = hg[SUBLANES - 1:SUBLANES, :]
    h_ref[...] = jnp.broadcast_to(h_prev, h_ref.shape)
    h = jnp.concatenate(hs, axis=0)

    out = h * jax.nn.gelu(y_ref[...].astype(F32))
    ms = jnp.mean(out * out, axis=-1, keepdims=True)
    o_ref[...] = (out * lax.rsqrt(ms + EPS) * nw_ref[...]).astype(o_ref.dtype)


def _lru(proj, conv_w, conv_b, w_gate, ba, bx, lam, norm_w, batch, seq, width, col0, tl):
    t = proj.shape[0]
    tl = _tile(seq, tl, SUBLANES)
    steps = seq // tl
    n_blocks = w_gate.shape[0]
    row_map = lambda col: (lambda b, s: (b * steps + s, col))
    vec = lambda r: pl.BlockSpec((r, width), lambda b, s: (0, 0))
    body = functools.partial(_lru_body, n_blocks=n_blocks, conv_w=conv_w.shape[0])
    return pl.pallas_call(
        body,
        grid=(batch, steps),
        in_specs=[pl.BlockSpec((tl, width), row_map(col0)),
                  pl.BlockSpec((tl, width), row_map(col0 + 1)),
                  vec(conv_w.shape[0]), vec(1),
                  pl.BlockSpec(w_gate.shape, lambda b, s: (0, 0, 0)),
                  vec(1), vec(1), vec(1), vec(1)],
        out_specs=pl.BlockSpec((tl, width), row_map(0)),
        out_shape=jax.ShapeDtypeStruct((t, width), BF16),
        scratch_shapes=[pltpu.VMEM((SUBLANES + tl, width), F32), pltpu.VMEM((SUBLANES, width), F32)],
        compiler_params=_cparams(("parallel", "arbitrary")),
        name="rglru",
    )(proj, proj, conv_w, conv_b, w_gate, ba, bx, lam, norm_w)


def _out_proj_body(a_ref, b_ref, wa_ref, wb_ref, x_ref, o_ref):
    acc = jnp.dot(a_ref[...], wa_ref[...], preferred_element_type=F32)
    acc = acc + jnp.dot(b_ref[...], wb_ref[...], preferred_element_type=F32)
    o_ref[...] = x_ref[...] + acc


def _out_proj(o_hg, o_lru, w_out, x, tm, tn):
    m, ka = o_hg.shape
    kb = o_lru.shape[1]
    n = w_out.shape[1]
    tm, tn = _tile(m, tm, SUBLANES), _tile(n, tn, LANES)
    assert ka == kb
    return pl.pallas_call(
        _out_proj_body,
        grid=(m // tm, n // tn),
        in_specs=[pl.BlockSpec((tm, ka), lambda i, j: (i, 0)),
                  pl.BlockSpec((tm, kb), lambda i, j: (i, 0)),
                  pl.BlockSpec((ka, tn), lambda i, j: (0, j)),
                  pl.BlockSpec((kb, tn), lambda i, j: (1, j)),
                  pl.BlockSpec((tm, tn), lambda i, j: (i, j))],
        out_specs=pl.BlockSpec((tm, tn), lambda i, j: (i, j)),
        out_shape=jax.ShapeDtypeStruct((m, n), F32),
        compiler_params=_cparams(("parallel", "arbitrary")),
        name="out_proj",
    )(o_hg, o_lru, w_out, w_out, x)


def _ffn_up_body(x_ref, wg_ref, wv_ref, cg_ref, cv_ref, bg_ref, bv_ref, wd_ref, o_ref, wdb_ref,
                 ag_ref, av_ref, ao_ref, xs_ref, *, rows_per_dot, conv_w):
    seq = x_ref.shape[0]
    tf = wg_ref.shape[1]
    halo = SUBLANES
    rd = rows_per_dot
    wdb_ref[...] = wd_ref[...].astype(wdb_ref.dtype)

    @pl.when(pl.program_id(1) == 0)
    def _():
        xs_ref[...] = x_ref[...]

    n_slab = tf // LANES
    ag_ref[:, 0:halo, :] = jnp.zeros((n_slab, halo, LANES), F32)
    av_ref[:, 0:halo, :] = jnp.zeros((n_slab, halo, LANES), F32)
    half = rd // 2

    def conv(acc_ref, s, r0, cw_ref, b_ref):
        lanes = slice(s * LANES, (s + 1) * LANES)
        reads = {o: acc_ref[s, pl.ds(halo + r0 + o, half, stride=2), :] for o in range(1 - conv_w, 2)}
        y_even = b_ref[:, lanes]
        y_odd = b_ref[:, lanes]
        for d in range(conv_w):
            w = cw_ref[conv_w - 1 - d:conv_w - d, lanes]
            y_even = y_even + reads[-d] * w
            y_odd = y_odd + reads[1 - d] * w
        return y_even, y_odd

    for r0 in range(0, seq, rd):
        xs = xs_ref[r0:r0 + rd, :]
        up_g = jnp.dot(xs, wg_ref[...], preferred_element_type=F32)
        up_v = jnp.dot(xs, wv_ref[...], preferred_element_type=F32)
        for s in range(n_slab):
            ag_ref[s, halo + r0:halo + r0 + rd, :] = up_g[:, s * LANES:(s + 1) * LANES]
            av_ref[s, halo + r0:halo + r0 + rd, :] = up_v[:, s * LANES:(s + 1) * LANES]
        for s in range(n_slab):
            g_even, g_odd = conv(ag_ref, s, r0, cg_ref, bg_ref)
            v_even, v_odd = conv(av_ref, s, r0, cv_ref, bv_ref)
            ao_ref[s, pl.ds(r0, half, stride=2), :] = _silu_of_half(g_even) * v_even
            ao_ref[s, pl.ds(r0 + 1, half, stride=2), :] = _silu_of_half(g_odd) * v_odd
            o_ref[r0:r0 + rd, s * LANES:(s + 1) * LANES] = ao_ref[s, r0:r0 + rd, :].astype(o_ref.dtype)


def _ffn_up(hn, w_up, conv_w, conv_b, w_down, batch, seq, d_ff, rows_per_dot):
    t, d = hn.shape
    tf = w_up.shape[2]
    nf = d_ff // tf
    assert w_up.shape == (2 * nf, d, tf)
    rows_per_dot = _tile(seq, rows_per_dot, SUBLANES)
    cw = conv_w.shape[0]
    body = functools.partial(_ffn_up_body, rows_per_dot=rows_per_dot, conv_w=cw)
    cast_in, cast_out, cast_shape = _cast_spec(w_down, nf, batch * nf)
    return pl.pallas_call(
        body,
        grid=(batch, nf),
        in_specs=[pl.BlockSpec((seq, d), lambda b, j: (b, 0), pipeline_mode=pl.Buffered(1)),
                  pl.BlockSpec((None, d, tf), lambda b, j: (j, 0, 0)),
                  pl.BlockSpec((None, d, tf), lambda b, j: (j + nf, 0, 0)),
                  pl.BlockSpec((cw, tf), lambda b, j: (0, j)),
                  pl.BlockSpec((cw, tf), lambda b, j: (0, j + nf)),
                  pl.BlockSpec((1, tf), lambda b, j: (0, j)),
                  pl.BlockSpec((1, tf), lambda b, j: (0, j + nf)),
                  cast_in],
        out_specs=[pl.BlockSpec((seq, tf), lambda b, j: (b, j)), cast_out],
        out_shape=[jax.ShapeDtypeStruct((t, d_ff), BF16), cast_shape],
        scratch_shapes=[pltpu.VMEM((tf // LANES, SUBLANES + seq, LANES), F32),
                        pltpu.VMEM((tf // LANES, SUBLANES + seq, LANES), F32),
                        pltpu.VMEM((tf // LANES, seq, LANES), F32),
                        pltpu.VMEM((seq, d), BF16)],
        compiler_params=_cparams(("arbitrary", "arbitrary")),
        name="ffn_up",
    )(hn, w_up, w_up, conv_w, conv_w, conv_b, conv_b, w_down)


def _ffn_down_body(a_ref, w_ref, h_ref, o_ref):
    o_ref[...] = h_ref[...] + jnp.dot(a_ref[...], w_ref[...], preferred_element_type=F32)


def _ffn_down(act, w_down, h, tm, tn):
    m, k = act.shape
    n = w_down.shape[1]
    tm, tn = _tile(m, tm, SUBLANES), _tile(n, tn, LANES)
    return pl.pallas_call(
        _ffn_down_body,
        grid=(n // tn, m // tm),
        in_specs=[pl.BlockSpec((tm, k), lambda j, i: (i, 0)),
                  pl.BlockSpec((k, tn), lambda j, i: (0, j), pipeline_mode=pl.Buffered(1)),
                  pl.BlockSpec((tm, tn), lambda j, i: (i, j))],
        out_specs=pl.BlockSpec((tm, tn), lambda j, i: (i, j)),
        out_shape=jax.ShapeDtypeStruct((m, n), F32),
        compiler_params=_cparams(("parallel", "arbitrary")),
        name="ffn_down",
    )(act, w_down, h)


def kernel(x, ln1_w, w_in, lb_gamma, hg_norm_w, lru_conv_w, lru_conv_b, lru_wa, lru_ba, lru_wx, lru_bx,
           lru_lambda, lru_norm_w, w_out, ln2_w, ffn_w_up, ffn_conv_w, ffn_conv_b, ffn_w_down,
           final_norm_w):
    batch, seq, d = x.shape
    depth = ln1_w.shape[0]
    hg_w = hg_norm_w.shape[1]
    lru_w = lru_lambda.shape[1]
    d_ff = ffn_w_down.shape[1]
    assert hg_w == lru_w and w_in.shape[2] == 4 * hg_w + 2 * lru_w
    t = batch * seq
    row = lambda v: v.reshape(1, -1).astype(F32)

    lb_all = jnp.cumsum(jax.nn.softmax(lb_gamma.astype(F32), axis=0), axis=0)
    h = x.reshape(t, d)
    for l in range(depth):
        hn = _rmsnorm(h, ln1_w[l], BF16, tm=512)
        tf = _tile(d_ff, 256, LANES)
        in_scale = jnp.concatenate([jnp.full((hg_w,), s, F32) for s in (0.5, 0.5, 1.0, 0.5)]
                                   + [jnp.ones((2 * lru_w,), F32)])
        w_in_b = (w_in[l].astype(F32) * in_scale).astype(BF16)
        proj, (w_out_b, w_up_b) = _matmul(hn, w_in_b, BF16, tm=1024, tn=1024,
                                          casts=((w_out[l].astype(F32), None), (ffn_w_up[l].astype(F32), tf)))
        o_hg = _hgrn(proj, row(lb_all[l]), row(hg_norm_w[l]), batch, seq, hg_w, ts=512)
        w_gate = (0.5 * jnp.concatenate([lru_wa[l], lru_wx[l]], axis=-1).astype(F32)).astype(BF16)
        o_lru = _lru(proj, lru_conv_w[l].astype(F32), row(lru_conv_b[l]), w_gate, 0.5 * row(lru_ba[l]),
                     0.5 * row(lru_bx[l]), row(lru_lambda[l]), row(lru_norm_w[l]), batch, seq, lru_w,
                     col0=4, tl=256)
        h = _out_proj(o_hg, o_lru, w_out_b, h, tm=1024, tn=1024)
        hn = _rmsnorm(h, ln2_w[l], BF16, tm=512)
        gate_scale = jnp.concatenate([jnp.full((d_ff,), 0.5, F32), jnp.ones((d_ff,), F32)])
        act, w_down_b = _ffn_up(hn, w_up_b, ffn_conv_w[l].astype(F32) * gate_scale,
                                row(ffn_conv_b[l]) * gate_scale, ffn_w_down[l].astype(F32),
                                batch, seq, d_ff, rows_per_dot=1024)
        h = _ffn_down(act, w_down_b, h, tm=512, tn=1024)
    out = _rmsnorm(h, final_norm_w, x.dtype, tm=512)
    return out.reshape(batch, seq, d)
```

```python
import functools

import jax
import jax.numpy as jnp
from jax import lax
from jax.experimental import pallas as pl
from jax.experimental.pallas import tpu as pltpu

EPS = 1e-6
LRU_C = 8.0
HEAD_DIM = 128
HG_CHUNK = 128
HG_GROUP = 8
LANES = 128
SUBLANES = 8
VMEM_LIMIT = 56 * 1024 * 1024
F32_MIN_NORMAL = 1.1754943508222875e-38
LOG2_E = 1.4426950408889634
GELU_C = 0.7978845608028654
FFN_HALO = SUBLANES

F32 = jnp.float32
BF16 = jnp.bfloat16


def _cparams(sem):
    return pltpu.CompilerParams(dimension_semantics=sem, vmem_limit_bytes=VMEM_LIMIT)


def _tile(dim, pref, unit):
    if dim <= pref:
        return dim
    t = (pref // unit) * unit
    while dim % t:
        t -= unit
    assert t > 0
    return t


def _neg_abs(x):
    bits = lax.bitcast_convert_type(x, jnp.int32) | jnp.int32(-2 ** 31)
    return lax.bitcast_convert_type(bits, F32)


def _silu_of_half(hx):
    return hx + hx * jnp.tanh(hx)


def _sigmoid_of_half(hx):
    return 0.5 + 0.5 * jnp.tanh(hx)


def _dot_nt(a, b):
    return lax.dot_general(a, b, (((1,), (1,)), ((), ())), preferred_element_type=F32)


def _dot_tn(a, b):
    return lax.dot_general(a, b, (((0,), (0,)), ((), ())), preferred_element_type=F32)


def _rmsnorm_body(x_ref, w_ref, o_ref):
    x = x_ref[...].astype(F32)
    ms = jnp.mean(x * x, axis=-1, keepdims=True)
    o_ref[...] = (x * lax.rsqrt(ms + EPS) * w_ref[...]).astype(o_ref.dtype)


def _rmsnorm(x, w, out_dtype, tm):
    m, d = x.shape
    tm = _tile(m, tm, SUBLANES)
    return pl.pallas_call(
        _rmsnorm_body,
        grid=(m // tm,),
        in_specs=[pl.BlockSpec((tm, d), lambda i: (i, 0)),
                  pl.BlockSpec((1, d), lambda i: (0, 0))],
        out_specs=pl.BlockSpec((tm, d), lambda i: (i, 0)),
        out_shape=jax.ShapeDtypeStruct((m, d), out_dtype),
        compiler_params=_cparams(("parallel",)),
        name="rmsnorm",
    )(x, w.reshape(1, d).astype(F32))


def _matmul_body(x_ref, w_ref, *refs):
    n_cast = (len(refs) - 1) // 2
    o_ref = refs[n_cast]
    o_ref[...] = jnp.dot(x_ref[...], w_ref[...], preferred_element_type=F32).astype(o_ref.dtype)
    for src, dst in zip(refs[:n_cast], refs[n_cast + 1:]):
        if len(dst.shape) == 2:
            dst[...] = src[...].astype(dst.dtype)
        else:
            ct = dst.shape[2]
            for c in range(dst.shape[0]):
                dst[c] = src[:, c * ct:(c + 1) * ct].astype(dst.dtype)


def _cast_spec(c, nj, steps, col_tile=None):
    rows, cols = c.shape
    rb = next(r for r in range(16, rows + 1, 16) if rows % r == 0 and rows // r <= steps)
    nb = rows // rb
    blk = lambda i, j: jnp.minimum(i * nj + j, nb - 1)
    src = pl.BlockSpec((rb, cols), lambda i, j: (blk(i, j), 0))
    if col_tile is None:
        return src, src, jax.ShapeDtypeStruct((rows, cols), BF16)
    nct = cols // col_tile
    dst = pl.BlockSpec((nct, rb, col_tile), lambda i, j: (0, blk(i, j), 0))
    return src, dst, jax.ShapeDtypeStruct((nct, rows, col_tile), BF16)


def _matmul(x, w, out_dtype, tm, tn, casts=()):
    m, k = x.shape
    n = w.shape[1]
    tm, tn = _tile(m, tm, SUBLANES), _tile(n, tn, LANES)
    ni, nj = m // tm, n // tn
    specs = [_cast_spec(c, nj, ni * nj, ct) for c, ct in casts]
    res = pl.pallas_call(
        _matmul_body,
        grid=(ni, nj),
        in_specs=[pl.BlockSpec((tm, k), lambda i, j: (i, 0)),
                  pl.BlockSpec((k, tn), lambda i, j: (0, j))] + [s[0] for s in specs],
        out_specs=[pl.BlockSpec((tm, tn), lambda i, j: (i, j))] + [s[1] for s in specs],
        out_shape=[jax.ShapeDtypeStruct((m, n), out_dtype)] + [s[2] for s in specs],
        compiler_params=_cparams(("arbitrary", "arbitrary")),
        name="in_proj",
    )(x, w, *[c for c, _ in casts])
    return res[0], res[1:]


def _hgrn_levels(c):
    halves = []
    h = c // 2
    while h >= 1:
        halves.append(h)
        h //= 2
    return halves


def _boundary_rows_in_vreg(b, half):
    c, d = b.shape
    blk = 2 * half
    b3 = b.reshape(c // SUBLANES, SUBLANES, d)
    row = lax.broadcasted_iota(jnp.int32, b3.shape, 1)
    out = None
    for r0 in range(0, SUBLANES, blk):
        p = r0 + half - 1
        cand = jnp.broadcast_to(b3[:, p:p + 1, :], b3.shape)
        out = cand if out is None else jnp.where(row >= r0, cand, out)
    return out.reshape(c, d)


def _hgrn_body(q_ref, f_ref, v_ref, g_ref, lb_ref, nw_ref, o_ref, st_ref, lvl_ref, *, n_heads, n_chunks):
    c = HG_CHUNK
    dk = HEAD_DIM
    halves = _hgrn_levels(c)
    n_lv = len(halves)
    nv = c // SUBLANES
    group = HG_GROUP if n_heads % HG_GROUP == 0 else 1
    n_groups = n_heads // group

    @pl.when(pl.program_id(1) == 0)
    def _():
        st_ref[...] = jnp.zeros_like(st_ref)

    ti = lax.broadcasted_iota(jnp.int32, (c, c), 0)
    si = lax.broadcasted_iota(jnp.int32, (c, c), 1)
    x = ti ^ si
    lvl = jnp.where(x == 0, -1, 0)
    for j in range(1, n_lv):
        lvl = lvl + (x >= (1 << j)).astype(jnp.int32)
    lvl_ref[...] = jnp.where(si > ti, -2, lvl)
    tril2 = jnp.concatenate([(si <= ti).astype(BF16)] * 2, axis=1)
    odd = (lax.broadcasted_iota(jnp.int32, (c, dk), 0) & 1) == 1

    def head_group(idx, carry):
        ci = idx // n_groups
        gi = idx % n_groups
        rows = pl.ds(pl.multiple_of(ci * c, c), c)
        heads = [gi * group + j for j in range(group)]
        cols = [pl.ds(pl.multiple_of(h * dk, dk), dk) for h in heads]

        qs, fs, ks, vs, b2s, qbs, kbs = [], [], [], [], [], [], []
        for cl in cols:
            lb = lb_ref[:, cl]
            q = _silu_of_half(q_ref[rows, cl].astype(F32))
            f = (0.5 + 0.5 * lb) + (0.5 - 0.5 * lb) * jnp.tanh(f_ref[rows, cl].astype(F32))
            k = 1.0 - f
            g2 = jnp.log2(f)
            g_hi = g2.astype(BF16)
            g_lo = (g2 - g_hi.astype(F32)).astype(BF16)
            b2s.append(jnp.dot(tril2, jnp.concatenate([g_hi, g_lo], axis=0), preferred_element_type=F32))
            qs.append(q)
            fs.append(f)
            ks.append(k)
            qbs.append(q.astype(BF16))
            kbs.append(k.astype(BF16))
            vs.append(v_ref[rows, cl])

        scs = []
        for qb, f, kb in zip(qbs, fs, kbs):
            s_d = _dot_nt(qb, kb)
            s_1 = _dot_nt(qb * jnp.where(odd, f, 1.0).astype(BF16), kb)
            sc = []
            for r in range(nv):
                rs = slice(r * SUBLANES, (r + 1) * SUBLANES)
                lv = lvl_ref[rs, :]
                sc.append(jnp.where(lv == 0, s_1[rs, :], jnp.where(lv == -1, s_d[rs, :], 0.0)))
            scs.append(sc)

        inters = []
        for h, qb, kb, v, b2 in zip(heads, qbs, kbs, vs, b2s):
            b_last = b2[c - 1:c, :]
            st = st_ref[h]
            inters.append(_dot_nt(qb * jnp.exp2(b2).astype(BF16), st.astype(BF16)))
            k_end = kb * jnp.exp2(b_last - b2).astype(BF16)
            st_ref[h] = jnp.exp2(b_last) * st + _dot_tn(v, k_end)

        for li, half in enumerate(halves[:-1]):
            level = n_lv - 1 - li
            blk = 2 * half
            prods = []
            for q, k, qb, kb, b2 in zip(qs, ks, qbs, kbs, b2s):
                if half >= 2 * SUBLANES:
                    lhs, rhs = [], []
                    for r0 in range(0, c, blk):
                        bp = b2[r0 + half - 1:r0 + half, :]
                        lo = slice(r0, r0 + half)
                        up = slice(r0 + half, r0 + blk)
                        lhs.append(qb[up, :] * jnp.exp2(b2[up, :] - bp).astype(BF16))
                        rhs.append(kb[lo, :] * jnp.exp2(bp - b2[lo, :]).astype(BF16))
                        rhs.append(kb[up, :])
                    lhs = lhs[0] if len(lhs) == 1 else jnp.concatenate(lhs, axis=0)
                    prods.append(_dot_nt(lhs, jnp.concatenate(rhs, axis=0)))
                elif half >= SUBLANES:
                    lhs, rhs = [], []
                    for r0 in range(0, c, blk):
                        bp = b2[r0 + half - 1:r0 + half, :]
                        lo = slice(r0, r0 + half)
                        up = slice(r0 + half, r0 + blk)
                        lhs.append(q[up, :] * jnp.exp2(b2[up, :] - bp))
                        rhs.append(k[lo, :] * jnp.exp2(bp - b2[lo, :]))
                        rhs.append(k[up, :])
                    lhs = lhs[0] if len(lhs) == 1 else jnp.concatenate(lhs, axis=0)
                    prods.append(_dot_nt(lhs.astype(BF16), jnp.concatenate(rhs, axis=0).astype(BF16)))
                else:
                    e = jnp.exp2(_neg_abs(b2 - _boundary_rows_in_vreg(b2, half))).astype(BF16)
                    prods.append(_dot_nt(qb * e, kb * e))
            if half >= SUBLANES:
                row_of = [r // SUBLANES for r0 in range(0, c, blk) for r in range(r0 + half, r0 + blk, SUBLANES)]
            else:
                row_of = list(range(nv))
            for sc, s_l in zip(scs, prods):
                for i, r in enumerate(row_of):
                    rs = slice(r * SUBLANES, (r + 1) * SUBLANES)
                    piece = s_l[i * SUBLANES:(i + 1) * SUBLANES, :]
                    sc[r] = jnp.where(lvl_ref[rs, :] == level, piece, sc[r])

        outs = []
        for sc, v, inter in zip(scs, vs, inters):
            scores = jnp.concatenate(sc, axis=0).astype(BF16)
            outs.append(inter + jnp.dot(scores, v, preferred_element_type=F32))
        for o, cl in zip(outs, cols):
            ms = jnp.mean(o * o, axis=-1, keepdims=True)
            o = o * lax.rsqrt(ms + EPS) * nw_ref[:, cl]
            o_ref[rows, cl] = (o * _silu_of_half(g_ref[rows, cl].astype(F32))).astype(o_ref.dtype)
        return carry

    lax.fori_loop(0, n_chunks * n_groups, head_group, 0)


def _hgrn(proj, lb, norm_w, batch, seq, width, ts):
    t = proj.shape[0]
    n_heads = width // HEAD_DIM
    ts = _tile(seq, ts, HG_CHUNK)
    n_chunks = ts // HG_CHUNK
    steps = seq // ts
    row_map = lambda col: (lambda b, s: (b * steps + s, col))
    body = functools.partial(_hgrn_body, n_heads=n_heads, n_chunks=n_chunks)
    return pl.pallas_call(
        body,
        grid=(batch, steps),
        in_specs=[pl.BlockSpec((ts, width), row_map(0)),
                  pl.BlockSpec((ts, width), row_map(1)),
                  pl.BlockSpec((ts, width), row_map(2)),
                  pl.BlockSpec((ts, width), row_map(3)),
                  pl.BlockSpec((1, width), lambda b, s: (0, 0)),
                  pl.BlockSpec((1, width), lambda b, s: (0, 0))],
        out_specs=pl.BlockSpec((ts, width), row_map(0)),
        out_shape=jax.ShapeDtypeStruct((t, width), BF16),
        scratch_shapes=[pltpu.VMEM((n_heads, HEAD_DIM, HEAD_DIM), F32),
                        pltpu.VMEM((HG_CHUNK, HG_CHUNK), jnp.int32)],
        compiler_params=_cparams(("parallel", "arbitrary")),
        name="hgrn2",
    )(proj, proj, proj, proj, lb, norm_w)


def _lru_body(x_ref, y_ref, cw_ref, cb_ref, wg_ref, ba_ref, bx_ref, lam_ref, nw_ref,
              o_ref, xs_ref, h_ref, *, n_blocks, conv_w):
    tl, width = x_ref.shape
    bd = width // n_blocks
    halo = SUBLANES
    first = pl.program_id(1) == 0

    @pl.when(first)
    def _():
        xs_ref[0:halo, :] = jnp.zeros((halo, width), F32)
        h_ref[...] = jnp.zeros_like(h_ref)

    xs_ref[halo:, :] = x_ref[...].astype(F32)
    xb = cb_ref[...] + xs_ref[halo:, :] * cw_ref[conv_w - 1:conv_w, :]
    for j in range(conv_w - 1):
        d = conv_w - 1 - j
        xb = xb + xs_ref[halo - d:halo - d + tl, :] * cw_ref[j:j + 1, :]
    xs_ref[0:halo, :] = xs_ref[tl:tl + halo, :]

    xbb = xb.astype(BF16)
    pre = [jnp.dot(xbb[:, n * bd:(n + 1) * bd], wg_ref[n], preferred_element_type=F32)
           for n in range(n_blocks)]
    pre_a = jnp.concatenate([p[:, :bd] for p in pre], axis=1)
    pre_x = jnp.concatenate([p[:, bd:] for p in pre], axis=1)
    r = _sigmoid_of_half(pre_a + ba_ref[...])
    gate_i = _sigmoid_of_half(pre_x + bx_ref[...])
    lam = lam_ref[...]
    softplus = jnp.maximum(-lam, 0.0) + jnp.log(1.0 + jnp.exp(-jnp.abs(lam)))
    neg_log_a = r * (LRU_C * softplus)
    a = jnp.exp2(r * ((-LRU_C * LOG2_E) * softplus))
    z = jnp.tanh(neg_log_a) * (1.0 + a * a)
    mult = z * lax.rsqrt(jnp.maximum(z, F32_MIN_NORMAL))
    row = lax.broadcasted_iota(jnp.int32, (SUBLANES, width), 0)
    head = jnp.where(jnp.logical_and(first, row == 0), 1.0, mult[:SUBLANES, :])
    mult = jnp.concatenate([head, mult[SUBLANES:, :]], axis=0)
    u = xb * gate_i * mult

    ng = tl // SUBLANES
    a3 = a.reshape(ng, SUBLANES, width)
    u3 = u.reshape(ng, SUBLANES, width)
    sub = lax.broadcasted_iota(jnp.int32, a3.shape, 1)
    d = 1
    while d < SUBLANES:
        u3 = a3 * jnp.where(sub < d, 0.0, pltpu.roll(u3, d, axis=1)) + u3
        a3 = a3 * jnp.where(sub < d, 1.0, pltpu.roll(a3, d, axis=1))
        d *= 2
    h_prev = h_ref[0:1, :]
    hs = []
    for gi in range(ng):
        hg = u3[gi] + a3[gi] * h_prev
        hs.append(hg)
        h_prev = hg[SUBLANES - 1:SUBLANES, :]
    h_ref[...] = jnp.broadcast_to(h_prev, h_ref.shape)
    h = jnp.concatenate(hs, axis=0)

    hy = y_ref[...].astype(F32)
    inner = hy * (2.0 * GELU_C + (8.0 * GELU_C * 0.044715) * (hy * hy))
    out = h * (hy + hy * jnp.tanh(inner))
    ms = jnp.mean(out * out, axis=-1, keepdims=True)
    o_ref[...] = (out * lax.rsqrt(ms + EPS) * nw_ref[...]).astype(o_ref.dtype)


def _lru(proj, conv_w, conv_b, w_gate, ba, bx, lam, norm_w, batch, seq, width, col0, tl):
    t = proj.shape[0]
    tl = _tile(seq, tl, SUBLANES)
    steps = seq // tl
    n_blocks = w_gate.shape[0]
    row_map = lambda col: (lambda b, s: (b * steps + s, col))
    vec = lambda r: pl.BlockSpec((r, width), lambda b, s: (0, 0))
    body = functools.partial(_lru_body, n_blocks=n_blocks, conv_w=conv_w.shape[0])
    return pl.pallas_call(
        body,
        grid=(batch, steps),
        in_specs=[pl.BlockSpec((tl, width), row_map(col0)),
                  pl.BlockSpec((tl, width), row_map(col0 + 1)),
                  vec(conv_w.shape[0]), vec(1),
                  pl.BlockSpec(w_gate.shape, lambda b, s: (0, 0, 0)),
                  vec(1), vec(1), vec(1), vec(1)],
        out_specs=pl.BlockSpec((tl, width), row_map(0)),
        out_shape=jax.ShapeDtypeStruct((t, width), BF16),
        scratch_shapes=[pltpu.VMEM((SUBLANES + tl, width), F32), pltpu.VMEM((SUBLANES, width), F32)],
        compiler_params=_cparams(("parallel", "arbitrary")),
        name="rglru",
    )(proj, proj, conv_w, conv_b, w_gate, ba, bx, lam, norm_w)


def _out_proj_body(a_ref, b_ref, wa_ref, wb_ref, x_ref, o_ref):
    acc = jnp.dot(a_ref[...], wa_ref[...], preferred_element_type=F32)
    acc = acc + jnp.dot(b_ref[...], wb_ref[...], preferred_element_type=F32)
    o_ref[...] = x_ref[...] + acc


def _out_proj(o_hg, o_lru, w_out, x, tm, tn):
    m, ka = o_hg.shape
    kb = o_lru.shape[1]
    n = w_out.shape[1]
    tm, tn = _tile(m, tm, SUBLANES), _tile(n, tn, LANES)
    assert ka == kb
    return pl.pallas_call(
        _out_proj_body,
        grid=(m // tm, n // tn),
        in_specs=[pl.BlockSpec((tm, ka), lambda i, j: (i, 0)),
                  pl.BlockSpec((tm, kb), lambda i, j: (i, 0)),
                  pl.BlockSpec((ka, tn), lambda i, j: (0, j)),
                  pl.BlockSpec((kb, tn), lambda i, j: (1, j)),
                  pl.BlockSpec((tm, tn), lambda i, j: (i, j))],
        out_specs=pl.BlockSpec((tm, tn), lambda i, j: (i, j)),
        out_shape=jax.ShapeDtypeStruct((m, n), F32),
        compiler_params=_cparams(("parallel", "arbitrary")),
        name="out_proj",
    )(o_hg, o_lru, w_out, w_out, x)


def _ffn_up_body(x_ref, wg_ref, wv_ref, cg_ref, cv_ref, bg_ref, bv_ref, wd_ref, o_ref, wdb_ref,
                 ag_ref, av_ref, ao_ref, xs_ref, *, row_tiles, conv_w):
    tf = wg_ref.shape[1]
    halo = FFN_HALO
    wdb_ref[...] = wd_ref[...].astype(wdb_ref.dtype)

    @pl.when(pl.program_id(1) == 0)
    def _():
        xs_ref[...] = x_ref[...]

    n_slab = tf // LANES
    ag_ref[:, 0:halo, :] = jnp.zeros((n_slab, halo, LANES), F32)
    av_ref[:, 0:halo, :] = jnp.zeros((n_slab, halo, LANES), F32)

    def conv(acc_ref, s, r0, half, cw_ref, b_ref):
        lanes = slice(s * LANES, (s + 1) * LANES)
        reads = {o: acc_ref[s, pl.ds(halo + r0 + o, half, stride=2), :] for o in range(1 - conv_w, 2)}
        y_even = b_ref[:, lanes]
        y_odd = b_ref[:, lanes]
        for d in range(conv_w):
            w = cw_ref[conv_w - 1 - d:conv_w - d, lanes]
            y_even = y_even + reads[-d] * w
            y_odd = y_odd + reads[1 - d] * w
        return y_even, y_odd

    r0 = 0
    for rd in row_tiles:
        half = rd // 2
        xs = xs_ref[r0:r0 + rd, :]
        up_g = jnp.dot(xs, wg_ref[...], preferred_element_type=F32)
        up_v = jnp.dot(xs, wv_ref[...], preferred_element_type=F32)
        for s in range(n_slab):
            ag_ref[s, halo + r0:halo + r0 + rd, :] = up_g[:, s * LANES:(s + 1) * LANES]
            av_ref[s, halo + r0:halo + r0 + rd, :] = up_v[:, s * LANES:(s + 1) * LANES]
        for s in range(n_slab):
            g_even, g_odd = conv(ag_ref, s, r0, half, cg_ref, bg_ref)
            v_even, v_odd = conv(av_ref, s, r0, half, cv_ref, bv_ref)
            ao_ref[s, pl.ds(r0, half, stride=2), :] = _silu_of_half(g_even) * v_even
            ao_ref[s, pl.ds(r0 + 1, half, stride=2), :] = _silu_of_half(g_odd) * v_odd
            o_ref[r0:r0 + rd, s * LANES:(s + 1) * LANES] = ao_ref[s, r0:r0 + rd, :].astype(o_ref.dtype)
        r0 += rd


def _ffn_row_tiles(seq, rows=1024):
    rows = _tile(seq, rows, 2 * SUBLANES)
    return (rows,) * (seq // rows)


def _ffn_up(hn, w_up, conv_w, conv_b, w_down, batch, seq, d_ff, row_tiles):
    t, d = hn.shape
    tf = w_up.shape[2]
    nf = d_ff // tf
    assert w_up.shape == (2 * nf, d, tf)
    assert sum(row_tiles) == seq and all(r % (2 * SUBLANES) == 0 for r in row_tiles)
    cw = conv_w.shape[0]
    body = functools.partial(_ffn_up_body, row_tiles=row_tiles, conv_w=cw)
    cast_in, cast_out, cast_shape = _cast_spec(w_down, nf, batch * nf)
    return pl.pallas_call(
        body,
        grid=(batch, nf),
        in_specs=[pl.BlockSpec((seq, d), lambda b, j: (b, 0), pipeline_mode=pl.Buffered(1)),
                  pl.BlockSpec((None, d, tf), lambda b, j: (j, 0, 0)),
                  pl.BlockSpec((None, d, tf), lambda b, j: (j + nf, 0, 0)),
                  pl.BlockSpec((cw, tf), lambda b, j: (0, j)),
                  pl.BlockSpec((cw, tf), lambda b, j: (0, j + nf)),
                  pl.BlockSpec((1, tf), lambda b, j: (0, j)),
                  pl.BlockSpec((1, tf), lambda b, j: (0, j + nf)),
                  cast_in],
        out_specs=[pl.BlockSpec((seq, tf), lambda b, j: (b, j)), cast_out],
        out_shape=[jax.ShapeDtypeStruct((t, d_ff), BF16), cast_shape],
        scratch_shapes=[pltpu.VMEM((tf // LANES, FFN_HALO + seq, LANES), F32),
                        pltpu.VMEM((tf // LANES, FFN_HALO + seq, LANES), F32),
                        pltpu.VMEM((tf // LANES, seq, LANES), F32),
                        pltpu.VMEM((seq, d), BF16)],
        compiler_params=_cparams(("arbitrary", "arbitrary")),
        name="ffn_up",
    )(hn, w_up, w_up, conv_w, conv_w, conv_b, conv_b, w_down)


def _ffn_down_body(a_ref, w_ref, h_ref, o_ref):
    o_ref[...] = h_ref[...] + jnp.dot(a_ref[...], w_ref[...], preferred_element_type=F32)


def _ffn_down(act, w_down, h, tm, tn):
    m, k = act.shape
    n = w_down.shape[1]
    tm, tn = _tile(m, tm, SUBLANES), _tile(n, tn, LANES)
    return pl.pallas_call(
        _ffn_down_body,
        grid=(n // tn, m // tm),
        in_specs=[pl.BlockSpec((tm, k), lambda j, i: (i, 0)),
                  pl.BlockSpec((k, tn), lambda j, i: (0, j), pipeline_mode=pl.Buffered(1)),
                  pl.BlockSpec((tm, tn), lambda j, i: (i, j))],
        out_specs=pl.BlockSpec((tm, tn), lambda j, i: (i, j)),
        out_shape=jax.ShapeDtypeStruct((m, n), F32),
        compiler_params=_cparams(("parallel", "arbitrary")),
        name="ffn_down",
    )(act, w_down, h)


def kernel(x, ln1_w, w_in, lb_gamma, hg_norm_w, lru_conv_w, lru_conv_b, lru_wa, lru_ba, lru_wx, lru_bx,
           lru_lambda, lru_norm_w, w_out, ln2_w, ffn_w_up, ffn_conv_w, ffn_conv_b, ffn_w_down,
           final_norm_w):
    batch, seq, d = x.shape
    depth = ln1_w.shape[0]
    hg_w = hg_norm_w.shape[1]
    lru_w = lru_lambda.shape[1]
    d_ff = ffn_w_down.shape[1]
    assert hg_w == lru_w and w_in.shape[2] == 4 * hg_w + 2 * lru_w
    t = batch * seq
    row = lambda v: v.reshape(1, -1).astype(F32)

    lb_all = jnp.cumsum(jax.nn.softmax(lb_gamma.astype(F32), axis=0), axis=0)
    h = x.reshape(t, d)
    for l in range(depth):
        tf = _tile(d_ff, 256, LANES)
        in_scale = jnp.concatenate([jnp.full((hg_w,), s, F32) for s in (0.5, 0.5, 1.0, 0.5)]
                                   + [jnp.full((lru_w,), s, F32) for s in (1.0, 0.5)])
        w_in_b = (w_in[l].astype(F32) * in_scale).astype(BF16)
        hn = _rmsnorm(h, ln1_w[l], BF16, tm=512)
        proj, (w_out_b, w_up_b) = _matmul(hn, w_in_b, BF16, tm=1024, tn=1024,
                                          casts=((w_out[l].astype(F32), None), (ffn_w_up[l].astype(F32), tf)))
        o_hg = _hgrn(proj, row(lb_all[l]), row(hg_norm_w[l]), batch, seq, hg_w, ts=512)
        w_gate = (0.5 * jnp.concatenate([lru_wa[l], lru_wx[l]], axis=-1).astype(F32)).astype(BF16)
        o_lru = _lru(proj, lru_conv_w[l].astype(F32), row(lru_conv_b[l]), w_gate, 0.5 * row(lru_ba[l]),
                     0.5 * row(lru_bx[l]), row(lru_lambda[l]), row(lru_norm_w[l]), batch, seq, lru_w,
                     col0=4, tl=256)
        h = _out_proj(o_hg, o_lru, w_out_b, h, tm=1024, tn=1024)
        hn = _rmsnorm(h, ln2_w[l], BF16, tm=512)
        gate_scale = jnp.concatenate([jnp.full((d_ff,), 0.5, F32), jnp.ones((d_ff,), F32)])
        act, w_down_b = _ffn_up(hn, w_up_b, ffn_conv_w[l].astype(F32) * gate_scale,
                                row(ffn_conv_b[l]) * gate_scale, ffn_w_down[l].astype(F32),
                                batch, seq, d_ff, row_tiles=_ffn_row_tiles(seq))
        h = _ffn_down(act, w_down_b, h, tm=512, tn=1024)
    out = _rmsnorm(h, final_norm_w, x.dtype, tm=512)
    return out.reshape(batch, seq, d)
```

```python
import functools

import jax
import jax.numpy as jnp
from jax import lax
from jax.experimental import pallas as pl
from jax.experimental.pallas import tpu as pltpu

EPS = 1e-6
LRU_C = 8.0
HEAD_DIM = 128
HG_CHUNK = 128
HG_GROUP = 8
LANES = 128
SUBLANES = 8
VMEM_LIMIT = 56 * 1024 * 1024
F32_MIN_NORMAL = 1.1754943508222875e-38
LOG2_E = 1.4426950408889634
GELU_C = 0.7978845608028654

NORM_ROWS = 512
IN_PROJ_TILE = (1024, 1024)
HGRN_ROWS = 512
LRU_ROWS = 256
OUT_PROJ_TILE = (1024, 1024)
FFN_COL_TILE = 256
FFN_ROW_TILE = 1024
FFN_DOWN_TILE = (512, 1024)
FFN_HALO = SUBLANES

F32 = jnp.float32
BF16 = jnp.bfloat16


def _cparams(sem):
    return pltpu.CompilerParams(dimension_semantics=sem, vmem_limit_bytes=VMEM_LIMIT)


def _tile(dim, pref, unit):
    if dim <= pref:
        return dim
    t = (pref // unit) * unit
    while dim % t:
        t -= unit
    assert t > 0
    return t


def _neg_abs(x):
    bits = lax.bitcast_convert_type(x, jnp.int32) | jnp.int32(-2 ** 31)
    return lax.bitcast_convert_type(bits, F32)


def _silu_of_half(hx):
    return hx + hx * jnp.tanh(hx)


def _sigmoid(x):
    return 1.0 / (1.0 + jnp.exp2(x * (-LOG2_E)))


def _sigmoid_of_half(hx):
    return 0.5 + 0.5 * jnp.tanh(hx)


def _dot_nt(a, b):
    return lax.dot_general(a, b, (((1,), (1,)), ((), ())), preferred_element_type=F32)


def _dot_tn(a, b):
    return lax.dot_general(a, b, (((0,), (0,)), ((), ())), preferred_element_type=F32)


def _rmsnorm_body(x_ref, w_ref, o_ref):
    x = x_ref[...].astype(F32)
    ms = jnp.mean(x * x, axis=-1, keepdims=True)
    o_ref[...] = (x * lax.rsqrt(ms + EPS) * w_ref[...]).astype(o_ref.dtype)


def _rmsnorm(x, w, out_dtype, tm):
    m, d = x.shape
    tm = _tile(m, tm, SUBLANES)
    return pl.pallas_call(
        _rmsnorm_body,
        grid=(m // tm,),
        in_specs=[pl.BlockSpec((tm, d), lambda i: (i, 0)),
                  pl.BlockSpec((1, d), lambda i: (0, 0))],
        out_specs=pl.BlockSpec((tm, d), lambda i: (i, 0)),
        out_shape=jax.ShapeDtypeStruct((m, d), out_dtype),
        compiler_params=_cparams(("parallel",)),
        name="rmsnorm",
    )(x, w.reshape(1, d).astype(F32))


def _matmul_body(x_ref, w_ref, *refs):
    n_cast = (len(refs) - 1) // 2
    o_ref = refs[n_cast]
    o_ref[...] = jnp.dot(x_ref[...], w_ref[...], preferred_element_type=F32).astype(o_ref.dtype)
    for src, dst in zip(refs[:n_cast], refs[n_cast + 1:]):
        if len(dst.shape) == 2:
            dst[...] = src[...].astype(dst.dtype)
        else:
            ct = dst.shape[2]
            for c in range(dst.shape[0]):
                dst[c] = src[:, c * ct:(c + 1) * ct].astype(dst.dtype)


def _cast_spec(c, nj, steps, col_tile=None):
    rows, cols = c.shape
    rb = next(r for r in range(16, rows + 1, 16) if rows % r == 0 and rows // r <= steps)
    nb = rows // rb
    blk = lambda i, j: jnp.minimum(i * nj + j, nb - 1)
    src = pl.BlockSpec((rb, cols), lambda i, j: (blk(i, j), 0))
    if col_tile is None:
        return src, src, jax.ShapeDtypeStruct((rows, cols), BF16)
    nct = cols // col_tile
    dst = pl.BlockSpec((nct, rb, col_tile), lambda i, j: (0, blk(i, j), 0))
    return src, dst, jax.ShapeDtypeStruct((nct, rows, col_tile), BF16)


def _matmul(x, w, out_dtype, tm, tn, casts=()):
    m, k = x.shape
    n = w.shape[1]
    tm, tn = _tile(m, tm, SUBLANES), _tile(n, tn, LANES)
    ni, nj = m // tm, n // tn
    specs = [_cast_spec(c, nj, ni * nj, ct) for c, ct in casts]
    res = pl.pallas_call(
        _matmul_body,
        grid=(ni, nj),
        in_specs=[pl.BlockSpec((tm, k), lambda i, j: (i, 0)),
                  pl.BlockSpec((k, tn), lambda i, j: (0, j))] + [s[0] for s in specs],
        out_specs=[pl.BlockSpec((tm, tn), lambda i, j: (i, j))] + [s[1] for s in specs],
        out_shape=[jax.ShapeDtypeStruct((m, n), out_dtype)] + [s[2] for s in specs],
        compiler_params=_cparams(("arbitrary", "arbitrary")),
        name="in_proj",
    )(x, w, *[c for c, _ in casts])
    return res[0], res[1:]


def _hgrn_levels(c):
    halves = []
    h = c // 2
    while h >= 1:
        halves.append(h)
        h //= 2
    return halves


def _boundary_rows_in_vreg(b, half):
    c, d = b.shape
    blk = 2 * half
    b3 = b.reshape(c // SUBLANES, SUBLANES, d)
    row = lax.broadcasted_iota(jnp.int32, b3.shape, 1)
    out = None
    for r0 in range(0, SUBLANES, blk):
        p = r0 + half - 1
        cand = jnp.broadcast_to(b3[:, p:p + 1, :], b3.shape)
        out = cand if out is None else jnp.where(row >= r0, cand, out)
    return out.reshape(c, d)


def _hgrn_body(q_ref, f_ref, v_ref, g_ref, lb_ref, nw_ref, o_ref, st_ref, lvl_ref, *, n_heads, n_chunks):
    c = HG_CHUNK
    dk = HEAD_DIM
    halves = _hgrn_levels(c)
    n_lv = len(halves)
    nv = c // SUBLANES
    group = HG_GROUP if n_heads % HG_GROUP == 0 else 1
    n_groups = n_heads // group

    @pl.when(pl.program_id(1) == 0)
    def _():
        st_ref[...] = jnp.zeros_like(st_ref)

    ti = lax.broadcasted_iota(jnp.int32, (c, c), 0)
    si = lax.broadcasted_iota(jnp.int32, (c, c), 1)
    x = ti ^ si
    lvl = jnp.where(x == 0, -1, 0)
    for j in range(1, n_lv):
        lvl = lvl + (x >= (1 << j)).astype(jnp.int32)
    lvl_ref[...] = jnp.where(si > ti, -2, lvl)
    tril2 = jnp.concatenate([(si <= ti).astype(BF16)] * 2, axis=1)
    odd = (lax.broadcasted_iota(jnp.int32, (c, dk), 0) & 1) == 1

    def head_group(idx, carry):
        ci = idx // n_groups
        gi = idx % n_groups
        rows = pl.ds(pl.multiple_of(ci * c, c), c)
        heads = [gi * group + j for j in range(group)]
        cols = [pl.ds(pl.multiple_of(h * dk, dk), dk) for h in heads]

        qs, fs, ks, vs, b2s, qbs, kbs = [], [], [], [], [], [], []
        for cl in cols:
            lb = lb_ref[:, cl]
            q = _silu_of_half(q_ref[rows, cl].astype(F32))
            f = lb + (1.0 - lb) * _sigmoid(f_ref[rows, cl].astype(F32))
            k = 1.0 - f
            g2 = jnp.log2(f)
            g_hi = g2.astype(BF16)
            g_lo = (g2 - g_hi.astype(F32)).astype(BF16)
            b2s.append(jnp.dot(tril2, jnp.concatenate([g_hi, g_lo], axis=0), preferred_element_type=F32))
            qs.append(q)
            fs.append(f)
            ks.append(k)
            qbs.append(q.astype(BF16))
            kbs.append(k.astype(BF16))
            vs.append(v_ref[rows, cl])

        scs = []
        for qb, f, kb in zip(qbs, fs, kbs):
            s_d = _dot_nt(qb, kb)
            s_1 = _dot_nt(qb * jnp.where(odd, f, 1.0).astype(BF16), kb)
            sc = []
            for r in range(nv):
                rs = slice(r * SUBLANES, (r + 1) * SUBLANES)
                lv = lvl_ref[rs, :]
                sc.append(jnp.where(lv == 0, s_1[rs, :], jnp.where(lv == -1, s_d[rs, :], 0.0)))
            scs.append(sc)

        inters = []
        for h, qb, kb, v, b2 in zip(heads, qbs, kbs, vs, b2s):
            b_last = b2[c - 1:c, :]
            st = st_ref[h]
            inters.append(_dot_nt(qb * jnp.exp2(b2).astype(BF16), st.astype(BF16)))
            k_end = kb * jnp.exp2(b_last - b2).astype(BF16)
            st_ref[h] = jnp.exp2(b_last) * st + _dot_tn(v, k_end)

        for li, half in enumerate(halves[:-1]):
            level = n_lv - 1 - li
            blk = 2 * half
            prods = []
            for q, k, qb, kb, b2 in zip(qs, ks, qbs, kbs, b2s):
                if half >= 2 * SUBLANES:
                    lhs, rhs = [], []
                    for r0 in range(0, c, blk):
                        bp = b2[r0 + half - 1:r0 + half, :]
                        lo = slice(r0, r0 + half)
                        up = slice(r0 + half, r0 + blk)
                        lhs.append(qb[up, :] * jnp.exp2(b2[up, :] - bp).astype(BF16))
                        rhs.append(kb[lo, :] * jnp.exp2(bp - b2[lo, :]).astype(BF16))
                        rhs.append(kb[up, :])
                    lhs = lhs[0] if len(lhs) == 1 else jnp.concatenate(lhs, axis=0)
                    prods.append(_dot_nt(lhs, jnp.concatenate(rhs, axis=0)))
                elif half >= SUBLANES:
                    lhs, rhs = [], []
                    for r0 in range(0, c, blk):
                        bp = b2[r0 + half - 1:r0 + half, :]
                        lo = slice(r0, r0 + half)
                        up = slice(r0 + half, r0 + blk)
                        lhs.append(q[up, :] * jnp.exp2(b2[up, :] - bp))
                        rhs.append(k[lo, :] * jnp.exp2(bp - b2[lo, :]))
                        rhs.append(k[up, :])
                    lhs = lhs[0] if len(lhs) == 1 else jnp.concatenate(lhs, axis=0)
                    prods.append(_dot_nt(lhs.astype(BF16), jnp.concatenate(rhs, axis=0).astype(BF16)))
                else:
                    e = jnp.exp2(_neg_abs(b2 - _boundary_rows_in_vreg(b2, half))).astype(BF16)
                    prods.append(_dot_nt(qb * e, kb * e))
            if half >= SUBLANES:
                row_of = [r // SUBLANES for r0 in range(0, c, blk) for r in range(r0 + half, r0 + blk, SUBLANES)]
            else:
                row_of = list(range(nv))
            for sc, s_l in zip(scs, prods):
                for i, r in enumerate(row_of):
                    rs = slice(r * SUBLANES, (r + 1) * SUBLANES)
                    piece = s_l[i * SUBLANES:(i + 1) * SUBLANES, :]
                    sc[r] = jnp.where(lvl_ref[rs, :] == level, piece, sc[r])

        outs = []
        for sc, v, inter in zip(scs, vs, inters):
            scores = jnp.concatenate(sc, axis=0).astype(BF16)
            outs.append(inter + jnp.dot(scores, v, preferred_element_type=F32))
        for o, cl in zip(outs, cols):
            ms = jnp.mean(o * o, axis=-1, keepdims=True)
            o = o * lax.rsqrt(ms + EPS) * nw_ref[:, cl]
            o_ref[rows, cl] = (o * _silu_of_half(g_ref[rows, cl].astype(F32))).astype(o_ref.dtype)
        return carry

    lax.fori_loop(0, n_chunks * n_groups, head_group, 0)


def _hgrn(proj, lb, norm_w, batch, seq, width, ts):
    t = proj.shape[0]
    n_heads = width // HEAD_DIM
    ts = _tile(seq, ts, HG_CHUNK)
    n_chunks = ts // HG_CHUNK
    steps = seq // ts
    row_map = lambda col: (lambda b, s: (b * steps + s, col))
    body = functools.partial(_hgrn_body, n_heads=n_heads, n_chunks=n_chunks)
    return pl.pallas_call(
        body,
        grid=(batch, steps),
        in_specs=[pl.BlockSpec((ts, width), row_map(0)),
                  pl.BlockSpec((ts, width), row_map(1)),
                  pl.BlockSpec((ts, width), row_map(2)),
                  pl.BlockSpec((ts, width), row_map(3)),
                  pl.BlockSpec((1, width), lambda b, s: (0, 0)),
                  pl.BlockSpec((1, width), lambda b, s: (0, 0))],
        out_specs=pl.BlockSpec((ts, width), row_map(0)),
        out_shape=jax.ShapeDtypeStruct((t, width), BF16),
        scratch_shapes=[pltpu.VMEM((n_heads, HEAD_DIM, HEAD_DIM), F32),
                        pltpu.VMEM((HG_CHUNK, HG_CHUNK), jnp.int32)],
        compiler_params=_cparams(("parallel", "arbitrary")),
        name="hgrn2",
    )(proj, proj, proj, proj, lb, norm_w)


def _lru_body(x_ref, y_ref, cw_ref, cb_ref, wg_ref, ba_ref, bx_ref, lam_ref, nw_ref,
              o_ref, xs_ref, h_ref, *, n_blocks, conv_w):
    tl, width = x_ref.shape
    bd = width // n_blocks
    halo = SUBLANES
    first = pl.program_id(1) == 0

    @pl.when(first)
    def _():
        xs_ref[0:halo, :] = jnp.zeros((halo, width), F32)
        h_ref[...] = jnp.zeros_like(h_ref)

    xs_ref[halo:, :] = x_ref[...].astype(F32)
    xb = cb_ref[...] + xs_ref[halo:, :] * cw_ref[conv_w - 1:conv_w, :]
    for j in range(conv_w - 1):
        d = conv_w - 1 - j
        xb = xb + xs_ref[halo - d:halo - d + tl, :] * cw_ref[j:j + 1, :]
    xs_ref[0:halo, :] = xs_ref[tl:tl + halo, :]

    xbb = xb.astype(BF16)
    pre = [jnp.dot(xbb[:, n * bd:(n + 1) * bd], wg_ref[n], preferred_element_type=F32)
           for n in range(n_blocks)]
    pre_a = jnp.concatenate([p[:, :bd] for p in pre], axis=1)
    pre_x = jnp.concatenate([p[:, bd:] for p in pre], axis=1)
    r = _sigmoid(pre_a + ba_ref[...])
    gate_i = _sigmoid_of_half(pre_x + bx_ref[...])
    lam = lam_ref[...]
    softplus = jnp.maximum(-lam, 0.0) + jnp.log(1.0 + jnp.exp(-jnp.abs(lam)))
    neg_log_a = r * (LRU_C * softplus)
    a = jnp.exp2(r * ((-LRU_C * LOG2_E) * softplus))
    z = jnp.tanh(neg_log_a) * (1.0 + a * a)
    mult = z * lax.rsqrt(jnp.maximum(z, F32_MIN_NORMAL))
    row = lax.broadcasted_iota(jnp.int32, (SUBLANES, width), 0)
    head = jnp.where(jnp.logical_and(first, row == 0), 1.0, mult[:SUBLANES, :])
    mult = jnp.concatenate([head, mult[SUBLANES:, :]], axis=0)
    u = xb * gate_i * mult

    ng = tl // SUBLANES
    a3 = a.reshape(ng, SUBLANES, width)
    u3 = u.reshape(ng, SUBLANES, width)
    sub = lax.broadcasted_iota(jnp.int32, a3.shape, 1)
    d = 1
    while d < SUBLANES:
        u3 = a3 * jnp.where(sub < d, 0.0, pltpu.roll(u3, d, axis=1)) + u3
        a3 = a3 * jnp.where(sub < d, 1.0, pltpu.roll(a3, d, axis=1))
        d *= 2
    h_prev = h_ref[0:1, :]
    hs = []
    for gi in range(ng):
        hg = u3[gi] + a3[gi] * h_prev
        hs.append(hg)
        h_prev = hg[SUBLANES - 1:SUBLANES, :]
    h_ref[...] = jnp.broadcast_to(h_prev, h_ref.shape)
    h = jnp.concatenate(hs, axis=0)

    hy = y_ref[...].astype(F32)
    inner = hy * (2.0 * GELU_C + (8.0 * GELU_C * 0.044715) * (hy * hy))
    out = h * (hy + hy * jnp.tanh(inner))
    ms = jnp.mean(out * out, axis=-1, keepdims=True)
    o_ref[...] = (out * lax.rsqrt(ms + EPS) * nw_ref[...]).astype(o_ref.dtype)


def _lru(proj, conv_w, conv_b, w_gate, ba, bx, lam, norm_w, batch, seq, width, col0, tl):
    t = proj.shape[0]
    tl = _tile(seq, tl, SUBLANES)
    steps = seq // tl
    n_blocks = w_gate.shape[0]
    row_map = lambda col: (lambda b, s: (b * steps + s, col))
    vec = lambda r: pl.BlockSpec((r, width), lambda b, s: (0, 0))
    body = functools.partial(_lru_body, n_blocks=n_blocks, conv_w=conv_w.shape[0])
    return pl.pallas_call(
        body,
        grid=(batch, steps),
        in_specs=[pl.BlockSpec((tl, width), row_map(col0)),
                  pl.BlockSpec((tl, width), row_map(col0 + 1)),
                  vec(conv_w.shape[0]), vec(1),
                  pl.BlockSpec(w_gate.shape, lambda b, s: (0, 0, 0)),
                  vec(1), vec(1), vec(1), vec(1)],
        out_specs=pl.BlockSpec((tl, width), row_map(0)),
        out_shape=jax.ShapeDtypeStruct((t, width), BF16),
        scratch_shapes=[pltpu.VMEM((SUBLANES + tl, width), F32), pltpu.VMEM((SUBLANES, width), F32)],
        compiler_params=_cparams(("parallel", "arbitrary")),
        name="rglru",
    )(proj, proj, conv_w, conv_b, w_gate, ba, bx, lam, norm_w)


def _out_proj_body(a_ref, b_ref, wa_ref, wb_ref, x_ref, o_ref):
    acc = jnp.dot(a_ref[...], wa_ref[...], preferred_element_type=F32)
    acc = acc + jnp.dot(b_ref[...], wb_ref[...], preferred_element_type=F32)
    o_ref[...] = x_ref[...] + acc


def _out_proj(o_hg, o_lru, w_out, x, tm, tn):
    m, ka = o_hg.shape
    kb = o_lru.shape[1]
    n = w_out.shape[1]
    tm, tn = _tile(m, tm, SUBLANES), _tile(n, tn, LANES)
    assert ka == kb
    return pl.pallas_call(
        _out_proj_body,
        grid=(m // tm, n // tn),
        in_specs=[pl.BlockSpec((tm, ka), lambda i, j: (i, 0)),
                  pl.BlockSpec((tm, kb), lambda i, j: (i, 0)),
                  pl.BlockSpec((ka, tn), lambda i, j: (0, j)),
                  pl.BlockSpec((kb, tn), lambda i, j: (1, j)),
                  pl.BlockSpec((tm, tn), lambda i, j: (i, j))],
        out_specs=pl.BlockSpec((tm, tn), lambda i, j: (i, j)),
        out_shape=jax.ShapeDtypeStruct((m, n), F32),
        compiler_params=_cparams(("parallel", "arbitrary")),
        name="out_proj",
    )(o_hg, o_lru, w_out, w_out, x)


def _ffn_up_body(x_ref, wg_ref, wv_ref, cg_ref, cv_ref, bg_ref, bv_ref, wd_ref, o_ref, wdb_ref,
                 ag_ref, av_ref, ao_ref, xs_ref, *, row_tiles, conv_w):
    tf = wg_ref.shape[1]
    halo = FFN_HALO
    wdb_ref[...] = wd_ref[...].astype(wdb_ref.dtype)

    @pl.when(pl.program_id(1) == 0)
    def _():
        xs_ref[...] = x_ref[...]

    n_slab = tf // LANES
    ag_ref[:, 0:halo, :] = jnp.zeros((n_slab, halo, LANES), F32)
    av_ref[:, 0:halo, :] = jnp.zeros((n_slab, halo, LANES), F32)

    def conv(acc_ref, s, r0, half, cw_ref, b_ref):
        lanes = slice(s * LANES, (s + 1) * LANES)
        reads = {o: acc_ref[s, pl.ds(halo + r0 + o, half, stride=2), :] for o in range(1 - conv_w, 2)}
        y_even = b_ref[:, lanes]
        y_odd = b_ref[:, lanes]
        for d in range(conv_w):
            w = cw_ref[conv_w - 1 - d:conv_w - d, lanes]
            y_even = y_even + reads[-d] * w
            y_odd = y_odd + reads[1 - d] * w
        return y_even, y_odd

    r0 = 0
    for rd in row_tiles:
        half = rd // 2
        xs = xs_ref[r0:r0 + rd, :]
        up_g = jnp.dot(xs, wg_ref[...], preferred_element_type=F32)
        up_v = jnp.dot(xs, wv_ref[...], preferred_element_type=F32)
        for s in range(n_slab):
            ag_ref[s, halo + r0:halo + r0 + rd, :] = up_g[:, s * LANES:(s + 1) * LANES]
            av_ref[s, halo + r0:halo + r0 + rd, :] = up_v[:, s * LANES:(s + 1) * LANES]
        for s in range(n_slab):
            g_even, g_odd = conv(ag_ref, s, r0, half, cg_ref, bg_ref)
            v_even, v_odd = conv(av_ref, s, r0, half, cv_ref, bv_ref)
            ao_ref[s, pl.ds(r0, half, stride=2), :] = _silu_of_half(g_even) * v_even
            ao_ref[s, pl.ds(r0 + 1, half, stride=2), :] = _silu_of_half(g_odd) * v_odd
            o_ref[r0:r0 + rd, s * LANES:(s + 1) * LANES] = ao_ref[s, r0:r0 + rd, :].astype(o_ref.dtype)
        r0 += rd


def _ffn_row_tiles(seq, rows=FFN_ROW_TILE):
    rows = _tile(seq, rows, 2 * SUBLANES)
    return (rows,) * (seq // rows)


def _ffn_up(hn, w_up, conv_w, conv_b, w_down, batch, seq, d_ff, row_tiles):
    t, d = hn.shape
    tf = w_up.shape[2]
    nf = d_ff // tf
    assert w_up.shape == (2 * nf, d, tf)
    assert sum(row_tiles) == seq and all(r % (2 * SUBLANES) == 0 for r in row_tiles)
    cw = conv_w.shape[0]
    body = functools.partial(_ffn_up_body, row_tiles=row_tiles, conv_w=cw)
    cast_in, cast_out, cast_shape = _cast_spec(w_down, nf, batch * nf)
    return pl.pallas_call(
        body,
        grid=(batch, nf),
        in_specs=[pl.BlockSpec((seq, d), lambda b, j: (b, 0), pipeline_mode=pl.Buffered(1)),
                  pl.BlockSpec((None, d, tf), lambda b, j: (j, 0, 0)),
                  pl.BlockSpec((None, d, tf), lambda b, j: (j + nf, 0, 0)),
                  pl.BlockSpec((cw, tf), lambda b, j: (0, j)),
                  pl.BlockSpec((cw, tf), lambda b, j: (0, j + nf)),
                  pl.BlockSpec((1, tf), lambda b, j: (0, j)),
                  pl.BlockSpec((1, tf), lambda b, j: (0, j + nf)),
                  cast_in],
        out_specs=[pl.BlockSpec((seq, tf), lambda b, j: (b, j)), cast_out],
        out_shape=[jax.ShapeDtypeStruct((t, d_ff), BF16), cast_shape],
        scratch_shapes=[pltpu.VMEM((tf // LANES, FFN_HALO + seq, LANES), F32),
                        pltpu.VMEM((tf // LANES, FFN_HALO + seq, LANES), F32),
                        pltpu.VMEM((tf // LANES, seq, LANES), F32),
                        pltpu.VMEM((seq, d), BF16)],
        compiler_params=_cparams(("arbitrary", "arbitrary")),
        name="ffn_up",
    )(hn, w_up, w_up, conv_w, conv_w, conv_b, conv_b, w_down)


def _ffn_down_body(a_ref, w_ref, h_ref, o_ref):
    o_ref[...] = h_ref[...] + jnp.dot(a_ref[...], w_ref[...], preferred_element_type=F32)


def _ffn_down(act, w_down, h, tm, tn):
    m, k = act.shape
    n = w_down.shape[1]
    tm, tn = _tile(m, tm, SUBLANES), _tile(n, tn, LANES)
    return pl.pallas_call(
        _ffn_down_body,
        grid=(n // tn, m // tm),
        in_specs=[pl.BlockSpec((tm, k), lambda j, i: (i, 0)),
                  pl.BlockSpec((k, tn), lambda j, i: (0, j), pipeline_mode=pl.Buffered(1)),
                  pl.BlockSpec((tm, tn), lambda j, i: (i, j))],
        out_specs=pl.BlockSpec((tm, tn), lambda j, i: (i, j)),
        out_shape=jax.ShapeDtypeStruct((m, n), F32),
        compiler_params=_cparams(("parallel", "arbitrary")),
        name="ffn_down",
    )(act, w_down, h)


def kernel(x, ln1_w, w_in, lb_gamma, hg_norm_w, lru_conv_w, lru_conv_b, lru_wa, lru_ba, lru_wx, lru_bx,
           lru_lambda, lru_norm_w, w_out, ln2_w, ffn_w_up, ffn_conv_w, ffn_conv_b, ffn_w_down,
           final_norm_w):
    batch, seq, d = x.shape
    depth = ln1_w.shape[0]
    hg_w = hg_norm_w.shape[1]
    lru_w = lru_lambda.shape[1]
    d_ff = ffn_w_down.shape[1]
    assert hg_w == lru_w and w_in.shape[2] == 4 * hg_w + 2 * lru_w
    t = batch * seq
    row = lambda v: v.reshape(1, -1).astype(F32)

    lb_all = jnp.cumsum(jax.nn.softmax(lb_gamma.astype(F32), axis=0), axis=0)
    h = x.reshape(t, d)
    for l in range(depth):
        tf = _tile(d_ff, FFN_COL_TILE, LANES)
        in_scale = jnp.concatenate([jnp.full((hg_w,), s, F32) for s in (0.5, 1.0, 1.0, 0.5)]
                                   + [jnp.full((lru_w,), s, F32) for s in (1.0, 0.5)])
        w_in_b = (w_in[l].astype(F32) * in_scale).astype(BF16)
        hn = _rmsnorm(h, ln1_w[l], BF16, tm=NORM_ROWS)
        proj, (w_out_b, w_up_b) = _matmul(hn, w_in_b, BF16, *IN_PROJ_TILE,
                                          casts=((w_out[l].astype(F32), None), (ffn_w_up[l].astype(F32), tf)))
        o_hg = _hgrn(proj, row(lb_all[l]), row(hg_norm_w[l]), batch, seq, hg_w, ts=HGRN_ROWS)
        w_gate = jnp.concatenate([lru_wa[l].astype(F32), 0.5 * lru_wx[l].astype(F32)], axis=-1).astype(BF16)
        o_lru = _lru(proj, lru_conv_w[l].astype(F32), row(lru_conv_b[l]), w_gate, row(lru_ba[l]),
                     0.5 * row(lru_bx[l]), row(lru_lambda[l]), row(lru_norm_w[l]), batch, seq, lru_w,
                     col0=4, tl=LRU_ROWS)
        h = _out_proj(o_hg, o_lru, w_out_b, h, *OUT_PROJ_TILE)
        hn = _rmsnorm(h, ln2_w[l], BF16, tm=NORM_ROWS)
        gate_scale = jnp.concatenate([jnp.full((d_ff,), 0.5, F32), jnp.ones((d_ff,), F32)])
        act, w_down_b = _ffn_up(hn, w_up_b, ffn_conv_w[l].astype(F32) * gate_scale,
                                row(ffn_conv_b[l]) * gate_scale, ffn_w_down[l].astype(F32),
                                batch, seq, d_ff, row_tiles=_ffn_row_tiles(seq))
        h = _ffn_down(act, w_down_b, h, *FFN_DOWN_TILE)
    out = _rmsnorm(h, final_norm_w, x.dtype, tm=NORM_ROWS)
    return out.reshape(batch, seq, d)
```

```python
import functools

import jax
import jax.numpy as jnp
from jax import lax
from jax.experimental import pallas as pl
from jax.experimental.pallas import tpu as pltpu

EPS = 1e-6
LRU_C = 8.0
HEAD_DIM = 128
HG_CHUNK = 128
HG_GROUP = 8
LANES = 128
SUBLANES = 8
VMEM_LIMIT = 56 * 1024 * 1024
F32_MIN_NORMAL = 1.1754943508222875e-38
LOG2_E = 1.4426950408889634
GELU_C = 0.7978845608028654

NORM_ROWS = 512
IN_PROJ_TILE = (1024, 1024)
HGRN_ROWS = 512
LRU_ROWS = 256
OUT_PROJ_TILE = (1024, 1024)
FFN_COL_TILE = 256
FFN_ROW_TILE = 1024
FFN_DOWN_TILE = (512, 1024)
FFN_HALO = SUBLANES

F32 = jnp.float32
BF16 = jnp.bfloat16


def _cparams(sem):
    return pltpu.CompilerParams(dimension_semantics=sem, vmem_limit_bytes=VMEM_LIMIT)


def _tile(dim, pref, unit):
    if dim <= pref:
        return dim
    t = (pref // unit) * unit
    while dim % t:
        t -= unit
    assert t > 0
    return t


def _neg_abs(x):
    bits = lax.bitcast_convert_type(x, jnp.int32) | jnp.int32(-2 ** 31)
    return lax.bitcast_convert_type(bits, F32)


def _silu_of_half(hx):
    return hx + hx * jnp.tanh(hx)


def _sigmoid(x):
    return 1.0 / (1.0 + jnp.exp2(x * (-LOG2_E)))


def _sigmoid_of_half(hx):
    return 0.5 + 0.5 * jnp.tanh(hx)


def _dot_nt(a, b):
    return lax.dot_general(a, b, (((1,), (1,)), ((), ())), preferred_element_type=F32)


def _dot_tn(a, b):
    return lax.dot_general(a, b, (((0,), (0,)), ((), ())), preferred_element_type=F32)


def _rmsnorm_body(x_ref, w_ref, o_ref):
    x = x_ref[...].astype(F32)
    ms = jnp.mean(x * x, axis=-1, keepdims=True)
    o_ref[...] = (x * lax.rsqrt(ms + EPS) * w_ref[...]).astype(o_ref.dtype)


def _rmsnorm(x, w, out_dtype, tm):
    m, d = x.shape
    tm = _tile(m, tm, SUBLANES)
    return pl.pallas_call(
        _rmsnorm_body,
        grid=(m // tm,),
        in_specs=[pl.BlockSpec((tm, d), lambda i: (i, 0)),
                  pl.BlockSpec((1, d), lambda i: (0, 0))],
        out_specs=pl.BlockSpec((tm, d), lambda i: (i, 0)),
        out_shape=jax.ShapeDtypeStruct((m, d), out_dtype),
        compiler_params=_cparams(("parallel",)),
        name="rmsnorm",
    )(x, w.reshape(1, d).astype(F32))


def _matmul_body(x_ref, w_ref, *refs):
    n_cast = (len(refs) - 1) // 2
    o_ref = refs[n_cast]
    o_ref[...] = jnp.dot(x_ref[...], w_ref[...], preferred_element_type=F32).astype(o_ref.dtype)
    for src, dst in zip(refs[:n_cast], refs[n_cast + 1:]):
        dst[...] = src[...].astype(dst.dtype)


def _cast_spec(c, nj, steps):
    rows, cols = c.shape
    rb = next(r for r in range(16, rows + 1, 16) if rows % r == 0 and rows // r <= steps)
    nb = rows // rb
    return pl.BlockSpec((rb, cols), lambda i, j: (jnp.minimum(i * nj + j, nb - 1), 0))


def _matmul(x, w, out_dtype, tm, tn, casts=()):
    m, k = x.shape
    n = w.shape[1]
    tm, tn = _tile(m, tm, SUBLANES), _tile(n, tn, LANES)
    ni, nj = m // tm, n // tn
    specs = [_cast_spec(c, nj, ni * nj) for c in casts]
    res = pl.pallas_call(
        _matmul_body,
        grid=(ni, nj),
        in_specs=[pl.BlockSpec((tm, k), lambda i, j: (i, 0)),
                  pl.BlockSpec((k, tn), lambda i, j: (0, j))] + specs,
        out_specs=[pl.BlockSpec((tm, tn), lambda i, j: (i, j))] + specs,
        out_shape=[jax.ShapeDtypeStruct((m, n), out_dtype)] + [jax.ShapeDtypeStruct(c.shape, BF16) for c in casts],
        compiler_params=_cparams(("arbitrary", "arbitrary")),
        name="in_proj",
    )(x, w, *casts)
    return res[0], res[1:]


def _hgrn_levels(c):
    halves = []
    h = c // 2
    while h >= 1:
        halves.append(h)
        h //= 2
    return halves


def _boundary_rows_in_vreg(b, half):
    c, d = b.shape
    blk = 2 * half
    b3 = b.reshape(c // SUBLANES, SUBLANES, d)
    row = lax.broadcasted_iota(jnp.int32, b3.shape, 1)
    out = None
    for r0 in range(0, SUBLANES, blk):
        p = r0 + half - 1
        cand = jnp.broadcast_to(b3[:, p:p + 1, :], b3.shape)
        out = cand if out is None else jnp.where(row >= r0, cand, out)
    return out.reshape(c, d)


def _hgrn_body(q_ref, f_ref, v_ref, g_ref, lb_ref, nw_ref, o_ref, st_ref, lvl_ref, *, n_heads, n_chunks):
    c = HG_CHUNK
    dk = HEAD_DIM
    halves = _hgrn_levels(c)
    n_lv = len(halves)
    nv = c // SUBLANES
    group = HG_GROUP if n_heads % HG_GROUP == 0 else 1
    n_groups = n_heads // group

    @pl.when(pl.program_id(1) == 0)
    def _():
        st_ref[...] = jnp.zeros_like(st_ref)

    ti = lax.broadcasted_iota(jnp.int32, (c, c), 0)
    si = lax.broadcasted_iota(jnp.int32, (c, c), 1)
    x = ti ^ si
    lvl = jnp.where(x == 0, -1, 0)
    for j in range(1, n_lv):
        lvl = lvl + (x >= (1 << j)).astype(jnp.int32)
    lvl_ref[...] = jnp.where(si > ti, -2, lvl)
    tril2 = jnp.concatenate([(si <= ti).astype(BF16)] * 2, axis=1)
    odd = (lax.broadcasted_iota(jnp.int32, (c, dk), 0) & 1) == 1

    def head_group(idx, carry):
        ci = idx // n_groups
        gi = idx % n_groups
        rows = pl.ds(pl.multiple_of(ci * c, c), c)
        heads = [gi * group + j for j in range(group)]
        cols = [pl.ds(pl.multiple_of(h * dk, dk), dk) for h in heads]

        qs, fs, ks, vs, b2s, qbs, kbs = [], [], [], [], [], [], []
        for cl in cols:
            lb = lb_ref[:, cl]
            q = _silu_of_half(q_ref[rows, cl].astype(F32))
            f = lb + (1.0 - lb) * _sigmoid(f_ref[rows, cl].astype(F32))
            k = 1.0 - f
            g2 = jnp.log2(f)
            g_hi = g2.astype(BF16)
            g_lo = (g2 - g_hi.astype(F32)).astype(BF16)
            b2s.append(jnp.dot(tril2, jnp.concatenate([g_hi, g_lo], axis=0), preferred_element_type=F32))
            qs.append(q)
            fs.append(f)
            ks.append(k)
            qbs.append(q.astype(BF16))
            kbs.append(k.astype(BF16))
            vs.append(v_ref[rows, cl])

        scs = []
        for qb, f, kb in zip(qbs, fs, kbs):
            s_d = _dot_nt(qb, kb)
            s_1 = _dot_nt(qb * jnp.where(odd, f, 1.0).astype(BF16), kb)
            sc = []
            for r in range(nv):
                rs = slice(r * SUBLANES, (r + 1) * SUBLANES)
                lv = lvl_ref[rs, :]
                sc.append(jnp.where(lv == 0, s_1[rs, :], jnp.where(lv == -1, s_d[rs, :], 0.0)))
            scs.append(sc)

        inters = []
        for h, qb, kb, v, b2 in zip(heads, qbs, kbs, vs, b2s):
            b_last = b2[c - 1:c, :]
            st = st_ref[h]
            inters.append(_dot_nt(qb * jnp.exp2(b2).astype(BF16), st.astype(BF16)))
            k_end = kb * jnp.exp2(b_last - b2).astype(BF16)
            st_ref[h] = jnp.exp2(b_last) * st + _dot_tn(v, k_end)

        for li, half in enumerate(halves[:-1]):
            level = n_lv - 1 - li
            blk = 2 * half
            prods = []
            for q, k, qb, kb, b2 in zip(qs, ks, qbs, kbs, b2s):
                if half >= 2 * SUBLANES:
                    lhs, rhs = [], []
                    for r0 in range(0, c, blk):
                        bp = b2[r0 + half - 1:r0 + half, :]
                        lo = slice(r0, r0 + half)
                        up = slice(r0 + half, r0 + blk)
                        lhs.append(qb[up, :] * jnp.exp2(b2[up, :] - bp).astype(BF16))
                        rhs.append(kb[lo, :] * jnp.exp2(bp - b2[lo, :]).astype(BF16))
                        rhs.append(kb[up, :])
                    lhs = lhs[0] if len(lhs) == 1 else jnp.concatenate(lhs, axis=0)
                    prods.append(_dot_nt(lhs, jnp.concatenate(rhs, axis=0)))
                elif half >= SUBLANES:
                    lhs, rhs = [], []
                    for r0 in range(0, c, blk):
                        bp = b2[r0 + half - 1:r0 + half, :]
                        lo = slice(r0, r0 + half)
                        up = slice(r0 + half, r0 + blk)
                        lhs.append(q[up, :] * jnp.exp2(b2[up, :] - bp))
                        rhs.append(k[lo, :] * jnp.exp2(bp - b2[lo, :]))
                        rhs.append(k[up, :])
                    lhs = lhs[0] if len(lhs) == 1 else jnp.concatenate(lhs, axis=0)
                    prods.append(_dot_nt(lhs.astype(BF16), jnp.concatenate(rhs, axis=0).astype(BF16)))
                else:
                    e = jnp.exp2(_neg_abs(b2 - _boundary_rows_in_vreg(b2, half))).astype(BF16)
                    prods.append(_dot_nt(qb * e, kb * e))
            if half >= SUBLANES:
                row_of = [r // SUBLANES for r0 in range(0, c, blk) for r in range(r0 + half, r0 + blk, SUBLANES)]
            else:
                row_of = list(range(nv))
            for sc, s_l in zip(scs, prods):
                for i, r in enumerate(row_of):
                    rs = slice(r * SUBLANES, (r + 1) * SUBLANES)
                    piece = s_l[i * SUBLANES:(i + 1) * SUBLANES, :]
                    sc[r] = jnp.where(lvl_ref[rs, :] == level, piece, sc[r])

        outs = []
        for sc, v, inter in zip(scs, vs, inters):
            scores = jnp.concatenate(sc, axis=0).astype(BF16)
            outs.append(inter + jnp.dot(scores, v, preferred_element_type=F32))
        for o, cl in zip(outs, cols):
            ms = jnp.mean(o * o, axis=-1, keepdims=True)
            o = o * lax.rsqrt(ms + EPS) * nw_ref[:, cl]
            o_ref[rows, cl] = (o * _silu_of_half(g_ref[rows, cl].astype(F32))).astype(o_ref.dtype)
        return carry

    lax.fori_loop(0, n_chunks * n_groups, head_group, 0)


def _hgrn(proj, lb, norm_w, batch, seq, width, ts):
    t = proj.shape[0]
    n_heads = width // HEAD_DIM
    ts = _tile(seq, ts, HG_CHUNK)
    n_chunks = ts // HG_CHUNK
    steps = seq // ts
    row_map = lambda col: (lambda b, s: (b * steps + s, col))
    body = functools.partial(_hgrn_body, n_heads=n_heads, n_chunks=n_chunks)
    return pl.pallas_call(
        body,
        grid=(batch, steps),
        in_specs=[pl.BlockSpec((ts, width), row_map(0)),
                  pl.BlockSpec((ts, width), row_map(1)),
                  pl.BlockSpec((ts, width), row_map(2)),
                  pl.BlockSpec((ts, width), row_map(3)),
                  pl.BlockSpec((1, width), lambda b, s: (0, 0)),
                  pl.BlockSpec((1, width), lambda b, s: (0, 0))],
        out_specs=pl.BlockSpec((ts, width), row_map(0)),
        out_shape=jax.ShapeDtypeStruct((t, width), BF16),
        scratch_shapes=[pltpu.VMEM((n_heads, HEAD_DIM, HEAD_DIM), F32),
                        pltpu.VMEM((HG_CHUNK, HG_CHUNK), jnp.int32)],
        compiler_params=_cparams(("parallel", "arbitrary")),
        name="hgrn2",
    )(proj, proj, proj, proj, lb, norm_w)


def _lru_body(x_ref, y_ref, cw_ref, cb_ref, wg_ref, ba_ref, bx_ref, lam_ref, nw_ref,
              o_ref, xs_ref, h_ref, *, n_blocks, conv_w):
    tl, width = x_ref.shape
    bd = width // n_blocks
    halo = SUBLANES
    first = pl.program_id(1) == 0

    @pl.when(first)
    def _():
        xs_ref[0:halo, :] = jnp.zeros((halo, width), F32)
        h_ref[...] = jnp.zeros_like(h_ref)

    xs_ref[halo:, :] = x_ref[...].astype(F32)
    xb = cb_ref[...] + xs_ref[halo:, :] * cw_ref[conv_w - 1:conv_w, :]
    for j in range(conv_w - 1):
        d = conv_w - 1 - j
        xb = xb + xs_ref[halo - d:halo - d + tl, :] * cw_ref[j:j + 1, :]
    xs_ref[0:halo, :] = xs_ref[tl:tl + halo, :]

    xbb = xb.astype(BF16)
    pre = [jnp.dot(xbb[:, n * bd:(n + 1) * bd], wg_ref[n], preferred_element_type=F32)
           for n in range(n_blocks)]
    pre_a = jnp.concatenate([p[:, :bd] for p in pre], axis=1)
    pre_x = jnp.concatenate([p[:, bd:] for p in pre], axis=1)
    r = _sigmoid(pre_a + ba_ref[...])
    gate_i = _sigmoid_of_half(pre_x + bx_ref[...])
    lam = lam_ref[...]
    softplus = jnp.maximum(-lam, 0.0) + jnp.log(1.0 + jnp.exp(-jnp.abs(lam)))
    neg_log_a = r * (LRU_C * softplus)
    a = jnp.exp2(r * ((-LRU_C * LOG2_E) * softplus))
    z = jnp.tanh(neg_log_a) * (1.0 + a * a)
    mult = z * lax.rsqrt(jnp.maximum(z, F32_MIN_NORMAL))
    row = lax.broadcasted_iota(jnp.int32, (SUBLANES, width), 0)
    head = jnp.where(jnp.logical_and(first, row == 0), 1.0, mult[:SUBLANES, :])
    mult = jnp.concatenate([head, mult[SUBLANES:, :]], axis=0)
    u = xb * gate_i * mult

    ng = tl // SUBLANES
    a3 = a.reshape(ng, SUBLANES, width)
    u3 = u.reshape(ng, SUBLANES, width)
    sub = lax.broadcasted_iota(jnp.int32, a3.shape, 1)
    d = 1
    while d < SUBLANES:
        u3 = a3 * jnp.where(sub < d, 0.0, pltpu.roll(u3, d, axis=1)) + u3
        a3 = a3 * jnp.where(sub < d, 1.0, pltpu.roll(a3, d, axis=1))
        d *= 2
    h_prev = h_ref[0:1, :]
    hs = []
    for gi in range(ng):
        hg = u3[gi] + a3[gi] * h_prev
        hs.append(hg)
        h_prev = hg[SUBLANES - 1:SUBLANES, :]
    h_ref[...] = jnp.broadcast_to(h_prev, h_ref.shape)
    h = jnp.concatenate(hs, axis=0)

    hy = y_ref[...].astype(F32)
    inner = hy * (2.0 * GELU_C + (8.0 * GELU_C * 0.044715) * (hy * hy))
    out = h * (hy + hy * jnp.tanh(inner))
    ms = jnp.mean(out * out, axis=-1, keepdims=True)
    o_ref[...] = (out * lax.rsqrt(ms + EPS) * nw_ref[...]).astype(o_ref.dtype)


def _lru(proj, conv_w, conv_b, w_gate, ba, bx, lam, norm_w, batch, seq, width, col0, tl):
    t = proj.shape[0]
    tl = _tile(seq, tl, SUBLANES)
    steps = seq // tl
    n_blocks = w_gate.shape[0]
    row_map = lambda col: (lambda b, s: (b * steps + s, col))
    vec = lambda r: pl.BlockSpec((r, width), lambda b, s: (0, 0))
    body = functools.partial(_lru_body, n_blocks=n_blocks, conv_w=conv_w.shape[0])
    return pl.pallas_call(
        body,
        grid=(batch, steps),
        in_specs=[pl.BlockSpec((tl, width), row_map(col0)),
                  pl.BlockSpec((tl, width), row_map(col0 + 1)),
                  vec(conv_w.shape[0]), vec(1),
                  pl.BlockSpec(w_gate.shape, lambda b, s: (0, 0, 0)),
                  vec(1), vec(1), vec(1), vec(1)],
        out_specs=pl.BlockSpec((tl, width), row_map(0)),
        out_shape=jax.ShapeDtypeStruct((t, width), BF16),
        scratch_shapes=[pltpu.VMEM((SUBLANES + tl, width), F32), pltpu.VMEM((SUBLANES, width), F32)],
        compiler_params=_cparams(("parallel", "arbitrary")),
        name="rglru",
    )(proj, proj, conv_w, conv_b, w_gate, ba, bx, lam, norm_w)


def _out_proj_body(a_ref, b_ref, wa_ref, wb_ref, x_ref, o_ref):
    acc = jnp.dot(a_ref[...], wa_ref[...], preferred_element_type=F32)
    acc = acc + jnp.dot(b_ref[...], wb_ref[...], preferred_element_type=F32)
    o_ref[...] = x_ref[...] + acc


def _out_proj(o_hg, o_lru, w_out, x, tm, tn):
    m, ka = o_hg.shape
    kb = o_lru.shape[1]
    n = w_out.shape[1]
    tm, tn = _tile(m, tm, SUBLANES), _tile(n, tn, LANES)
    assert ka == kb
    return pl.pallas_call(
        _out_proj_body,
        grid=(m // tm, n // tn),
        in_specs=[pl.BlockSpec((tm, ka), lambda i, j: (i, 0)),
                  pl.BlockSpec((tm, kb), lambda i, j: (i, 0)),
                  pl.BlockSpec((ka, tn), lambda i, j: (0, j)),
                  pl.BlockSpec((kb, tn), lambda i, j: (1, j)),
                  pl.BlockSpec((tm, tn), lambda i, j: (i, j))],
        out_specs=pl.BlockSpec((tm, tn), lambda i, j: (i, j)),
        out_shape=jax.ShapeDtypeStruct((m, n), F32),
        compiler_params=_cparams(("parallel", "arbitrary")),
        name="out_proj",
    )(o_hg, o_lru, w_out, w_out, x)


def _ffn_up_body(x_ref, wg_ref, wv_ref, cg_ref, cv_ref, bg_ref, bv_ref, wd_ref, o_ref, wdb_ref,
                 ag_ref, av_ref, ao_ref, xs_ref, *, row_tiles, conv_w):
    tf = wg_ref.shape[1]
    halo = FFN_HALO
    wdb_ref[...] = wd_ref[...].astype(wdb_ref.dtype)

    @pl.when(pl.program_id(1) == 0)
    def _():
        xs_ref[...] = x_ref[...]

    n_slab = tf // LANES
    ag_ref[:, 0:halo, :] = jnp.zeros((n_slab, halo, LANES), F32)
    av_ref[:, 0:halo, :] = jnp.zeros((n_slab, halo, LANES), F32)

    def conv(acc_ref, s, r0, half, cw_ref, b_ref):
        lanes = slice(s * LANES, (s + 1) * LANES)
        reads = {o: acc_ref[s, pl.ds(halo + r0 + o, half, stride=2), :] for o in range(1 - conv_w, 2)}
        y_even = b_ref[:, lanes]
        y_odd = b_ref[:, lanes]
        for d in range(conv_w):
            w = cw_ref[conv_w - 1 - d:conv_w - d, lanes]
            y_even = y_even + reads[-d] * w
            y_odd = y_odd + reads[1 - d] * w
        return y_even, y_odd

    r0 = 0
    for rd in row_tiles:
        half = rd // 2
        xs = xs_ref[r0:r0 + rd, :]
        up_g = jnp.dot(xs, wg_ref[...], preferred_element_type=F32)
        up_v = jnp.dot(xs, wv_ref[...], preferred_element_type=F32)
        for s in range(n_slab):
            ag_ref[s, halo + r0:halo + r0 + rd, :] = up_g[:, s * LANES:(s + 1) * LANES]
            av_ref[s, halo + r0:halo + r0 + rd, :] = up_v[:, s * LANES:(s + 1) * LANES]
        for s in range(n_slab):
            g_even, g_odd = conv(ag_ref, s, r0, half, cg_ref, bg_ref)
            v_even, v_odd = conv(av_ref, s, r0, half, cv_ref, bv_ref)
            ao_ref[s, pl.ds(r0, half, stride=2), :] = _silu_of_half(g_even) * v_even
            ao_ref[s, pl.ds(r0 + 1, half, stride=2), :] = _silu_of_half(g_odd) * v_odd
            o_ref[r0:r0 + rd, s * LANES:(s + 1) * LANES] = ao_ref[s, r0:r0 + rd, :].astype(o_ref.dtype)
        r0 += rd


def _ffn_row_tiles(seq, rows=FFN_ROW_TILE):
    rows = _tile(seq, rows, 2 * SUBLANES)
    return (rows,) * (seq // rows)


def _ffn_up(hn, w_up, conv_w, conv_b, w_down, batch, seq, d_ff, tf, row_tiles):
    t, d = hn.shape
    tf = _tile(d_ff, tf, LANES)
    nf = d_ff // tf
    assert sum(row_tiles) == seq and all(r % (2 * SUBLANES) == 0 for r in row_tiles)
    cw = conv_w.shape[0]
    body = functools.partial(_ffn_up_body, row_tiles=row_tiles, conv_w=cw)
    cast_spec = _cast_spec(w_down, nf, batch * nf)
    return pl.pallas_call(
        body,
        grid=(batch, nf),
        in_specs=[pl.BlockSpec((seq, d), lambda b, j: (b, 0), pipeline_mode=pl.Buffered(1)),
                  pl.BlockSpec((d, tf), lambda b, j: (0, j)),
                  pl.BlockSpec((d, tf), lambda b, j: (0, j + nf)),
                  pl.BlockSpec((cw, tf), lambda b, j: (0, j)),
                  pl.BlockSpec((cw, tf), lambda b, j: (0, j + nf)),
                  pl.BlockSpec((1, tf), lambda b, j: (0, j)),
                  pl.BlockSpec((1, tf), lambda b, j: (0, j + nf)),
                  cast_spec],
        out_specs=[pl.BlockSpec((seq, tf), lambda b, j: (b, j)), cast_spec],
        out_shape=[jax.ShapeDtypeStruct((t, d_ff), BF16), jax.ShapeDtypeStruct(w_down.shape, BF16)],
        scratch_shapes=[pltpu.VMEM((tf // LANES, FFN_HALO + seq, LANES), F32),
                        pltpu.VMEM((tf // LANES, FFN_HALO + seq, LANES), F32),
                        pltpu.VMEM((tf // LANES, seq, LANES), F32),
                        pltpu.VMEM((seq, d), BF16)],
        compiler_params=_cparams(("arbitrary", "arbitrary")),
        name="ffn_up",
    )(hn, w_up, w_up, conv_w, conv_w, conv_b, conv_b, w_down)


def _ffn_down_body(a_ref, w_ref, h_ref, o_ref):
    o_ref[...] = h_ref[...] + jnp.dot(a_ref[...], w_ref[...], preferred_element_type=F32)


def _ffn_down(act, w_down, h, tm, tn):
    m, k = act.shape
    n = w_down.shape[1]
    tm, tn = _tile(m, tm, SUBLANES), _tile(n, tn, LANES)
    return pl.pallas_call(
        _ffn_down_body,
        grid=(n // tn, m // tm),
        in_specs=[pl.BlockSpec((tm, k), lambda j, i: (i, 0)),
                  pl.BlockSpec((k, tn), lambda j, i: (0, j), pipeline_mode=pl.Buffered(1)),
                  pl.BlockSpec((tm, tn), lambda j, i: (i, j))],
        out_specs=pl.BlockSpec((tm, tn), lambda j, i: (i, j)),
        out_shape=jax.ShapeDtypeStruct((m, n), F32),
        compiler_params=_cparams(("parallel", "arbitrary")),
        name="ffn_down",
    )(act, w_down, h)


def kernel(x, ln1_w, w_in, lb_gamma, hg_norm_w, lru_conv_w, lru_conv_b, lru_wa, lru_ba, lru_wx, lru_bx,
           lru_lambda, lru_norm_w, w_out, ln2_w, ffn_w_up, ffn_conv_w, ffn_conv_b, ffn_w_down,
           final_norm_w):
    batch, seq, d = x.shape
    depth = ln1_w.shape[0]
    hg_w = hg_norm_w.shape[1]
    lru_w = lru_lambda.shape[1]
    d_ff = ffn_w_down.shape[1]
    assert hg_w == lru_w and w_in.shape[2] == 4 * hg_w + 2 * lru_w
    t = batch * seq
    row = lambda v: v.reshape(1, -1).astype(F32)

    lb_all = jnp.cumsum(jax.nn.softmax(lb_gamma.astype(F32), axis=0), axis=0)
    h = x.reshape(t, d)
    for l in range(depth):
        in_scale = jnp.concatenate([jnp.full((hg_w,), s, F32) for s in (0.5, 1.0, 1.0, 0.5)]
                                   + [jnp.full((lru_w,), s, F32) for s in (1.0, 0.5)])
        w_in_b = (w_in[l].astype(F32) * in_scale).astype(BF16)
        hn = _rmsnorm(h, ln1_w[l], BF16, tm=NORM_ROWS)
        proj, (w_out_b, w_up_b) = _matmul(hn, w_in_b, BF16, *IN_PROJ_TILE,
                                          casts=(w_out[l].astype(F32), ffn_w_up[l].astype(F32)))
        o_hg = _hgrn(proj, row(lb_all[l]), row(hg_norm_w[l]), batch, seq, hg_w, ts=HGRN_ROWS)
        w_gate = jnp.concatenate([lru_wa[l].astype(F32), 0.5 * lru_wx[l].astype(F32)], axis=-1).astype(BF16)
        o_lru = _lru(proj, lru_conv_w[l].astype(F32), row(lru_conv_b[l]), w_gate, row(lru_ba[l]),
                     0.5 * row(lru_bx[l]), row(lru_lambda[l]), row(lru_norm_w[l]), batch, seq, lru_w,
                     col0=4, tl=LRU_ROWS)
        h = _out_proj(o_hg, o_lru, w_out_b, h, *OUT_PROJ_TILE)
        hn = _rmsnorm(h, ln2_w[l], BF16, tm=NORM_ROWS)
        gate_scale = jnp.concatenate([jnp.full((d_ff,), 0.5, F32), jnp.ones((d_ff,), F32)])
        act, w_down_b = _ffn_up(hn, w_up_b, ffn_conv_w[l].astype(F32) * gate_scale,
                                row(ffn_conv_b[l]) * gate_scale, ffn_w_down[l].astype(F32),
                                batch, seq, d_ff, tf=FFN_COL_TILE, row_tiles=_ffn_row_tiles(seq))
        h = _ffn_down(act, w_down_b, h, *FFN_DOWN_TILE)
    out = _rmsnorm(h, final_norm_w, x.dtype, tm=NORM_ROWS)
    return out.reshape(batch, seq, d)
```

```python
import functools

import jax
import jax.numpy as jnp
from jax import lax
from jax.experimental import pallas as pl
from jax.experimental.pallas import tpu as pltpu

EPS = 1e-6
LRU_C = 8.0
HEAD_DIM = 128
HG_CHUNK = 128
HG_GROUP = 8
LANES = 128
SUBLANES = 8
VMEM_LIMIT = 56 * 1024 * 1024
F32_MIN_NORMAL = 1.1754943508222875e-38
LOG2_E = 1.4426950408889634
GELU_C = 0.7978845608028654

NORM_ROWS = 512
IN_PROJ_TILE = (1024, 1024)
HGRN_ROWS = 512
LRU_ROWS = 256
OUT_PROJ_ROWS = 128
FFN_COL_TILE = 256
FFN_ROW_TILE = 1024
FFN_DOWN_TILE = (512, 1024)
FFN_HALO = SUBLANES

F32 = jnp.float32
BF16 = jnp.bfloat16


def _cparams(sem):
    return pltpu.CompilerParams(dimension_semantics=sem, vmem_limit_bytes=VMEM_LIMIT)


def _tile(dim, pref, unit):
    if dim <= pref:
        return dim
    t = (pref // unit) * unit
    while dim % t:
        t -= unit
    assert t > 0
    return t


def _neg_abs(x):
    bits = lax.bitcast_convert_type(x, jnp.int32) | jnp.int32(-2 ** 31)
    return lax.bitcast_convert_type(bits, F32)


def _silu_of_half(hx):
    return hx + hx * jnp.tanh(hx)


def _sigmoid(x):
    return 1.0 / (1.0 + jnp.exp2(x * (-LOG2_E)))


def _sigmoid_of_half(hx):
    return 0.5 + 0.5 * jnp.tanh(hx)


def _dot_nt(a, b):
    return lax.dot_general(a, b, (((1,), (1,)), ((), ())), preferred_element_type=F32)


def _dot_tn(a, b):
    return lax.dot_general(a, b, (((0,), (0,)), ((), ())), preferred_element_type=F32)


def _rmsnorm_body(x_ref, w_ref, o_ref):
    x = x_ref[...].astype(F32)
    ms = jnp.mean(x * x, axis=-1, keepdims=True)
    o_ref[...] = (x * lax.rsqrt(ms + EPS) * w_ref[...]).astype(o_ref.dtype)


def _rmsnorm(x, w, out_dtype, tm):
    m, d = x.shape
    tm = _tile(m, tm, SUBLANES)
    return pl.pallas_call(
        _rmsnorm_body,
        grid=(m // tm,),
        in_specs=[pl.BlockSpec((tm, d), lambda i: (i, 0)),
                  pl.BlockSpec((1, d), lambda i: (0, 0))],
        out_specs=pl.BlockSpec((tm, d), lambda i: (i, 0)),
        out_shape=jax.ShapeDtypeStruct((m, d), out_dtype),
        compiler_params=_cparams(("parallel",)),
        name="rmsnorm",
    )(x, w.reshape(1, d).astype(F32))


def _matmul_body(x_ref, w_ref, *refs):
    n_cast = (len(refs) - 1) // 2
    o_ref = refs[n_cast]
    o_ref[...] = jnp.dot(x_ref[...], w_ref[...], preferred_element_type=F32).astype(o_ref.dtype)
    for src, dst in zip(refs[:n_cast], refs[n_cast + 1:]):
        dst[...] = src[...].astype(dst.dtype)


def _cast_spec(c, nj, steps):
    rows, cols = c.shape
    rb = next(r for r in range(16, rows + 1, 16) if rows % r == 0 and rows // r <= steps)
    nb = rows // rb
    return pl.BlockSpec((rb, cols), lambda i, j: (jnp.minimum(i * nj + j, nb - 1), 0))


def _matmul(x, w, out_dtype, tm, tn, casts=()):
    m, k = x.shape
    n = w.shape[1]
    tm, tn = _tile(m, tm, SUBLANES), _tile(n, tn, LANES)
    ni, nj = m // tm, n // tn
    specs = [_cast_spec(c, nj, ni * nj) for c in casts]
    res = pl.pallas_call(
        _matmul_body,
        grid=(ni, nj),
        in_specs=[pl.BlockSpec((tm, k), lambda i, j: (i, 0)),
                  pl.BlockSpec((k, tn), lambda i, j: (0, j))] + specs,
        out_specs=[pl.BlockSpec((tm, tn), lambda i, j: (i, j))] + specs,
        out_shape=[jax.ShapeDtypeStruct((m, n), out_dtype)] + [jax.ShapeDtypeStruct(c.shape, BF16) for c in casts],
        compiler_params=_cparams(("arbitrary", "arbitrary")),
        name="in_proj",
    )(x, w, *casts)
    return res[0], res[1:]


def _hgrn_levels(c):
    halves = []
    h = c // 2
    while h >= 1:
        halves.append(h)
        h //= 2
    return halves


def _boundary_rows_in_vreg(b, half):
    c, d = b.shape
    blk = 2 * half
    b3 = b.reshape(c // SUBLANES, SUBLANES, d)
    row = lax.broadcasted_iota(jnp.int32, b3.shape, 1)
    out = None
    for r0 in range(0, SUBLANES, blk):
        p = r0 + half - 1
        cand = jnp.broadcast_to(b3[:, p:p + 1, :], b3.shape)
        out = cand if out is None else jnp.where(row >= r0, cand, out)
    return out.reshape(c, d)


def _hgrn_body(q_ref, f_ref, v_ref, g_ref, lb_ref, nw_ref, o_ref, st_ref, lvl_ref, *, n_heads, n_chunks):
    c = HG_CHUNK
    dk = HEAD_DIM
    halves = _hgrn_levels(c)
    n_lv = len(halves)
    nv = c // SUBLANES
    group = HG_GROUP if n_heads % HG_GROUP == 0 else 1
    n_groups = n_heads // group

    @pl.when(pl.program_id(1) == 0)
    def _():
        st_ref[...] = jnp.zeros_like(st_ref)

    ti = lax.broadcasted_iota(jnp.int32, (c, c), 0)
    si = lax.broadcasted_iota(jnp.int32, (c, c), 1)
    x = ti ^ si
    lvl = jnp.where(x == 0, -1, 0)
    for j in range(1, n_lv):
        lvl = lvl + (x >= (1 << j)).astype(jnp.int32)
    lvl_ref[...] = jnp.where(si > ti, -2, lvl)
    tril2 = jnp.concatenate([(si <= ti).astype(BF16)] * 2, axis=1)
    odd = (lax.broadcasted_iota(jnp.int32, (c, dk), 0) & 1) == 1

    def head_group(idx, carry):
        ci = idx // n_groups
        gi = idx % n_groups
        rows = pl.ds(pl.multiple_of(ci * c, c), c)
        heads = [gi * group + j for j in range(group)]
        cols = [pl.ds(pl.multiple_of(h * dk, dk), dk) for h in heads]

        qs, fs, ks, vs, b2s, qbs, kbs = [], [], [], [], [], [], []
        for cl in cols:
            lb = lb_ref[:, cl]
            q = _silu_of_half(q_ref[rows, cl].astype(F32))
            f = lb + (1.0 - lb) * _sigmoid(f_ref[rows, cl].astype(F32))
            k = 1.0 - f
            g2 = jnp.log2(f)
            g_hi = g2.astype(BF16)
            g_lo = (g2 - g_hi.astype(F32)).astype(BF16)
            b2s.append(jnp.dot(tril2, jnp.concatenate([g_hi, g_lo], axis=0), preferred_element_type=F32))
            qs.append(q)
            fs.append(f)
            ks.append(k)
            qbs.append(q.astype(BF16))
            kbs.append(k.astype(BF16))
            vs.append(v_ref[rows, cl])

        scs = []
        for qb, f, kb in zip(qbs, fs, kbs):
            s_d = _dot_nt(qb, kb)
            s_1 = _dot_nt(qb * jnp.where(odd, f, 1.0).astype(BF16), kb)
            sc = []
            for r in range(nv):
                rs = slice(r * SUBLANES, (r + 1) * SUBLANES)
                lv = lvl_ref[rs, :]
                sc.append(jnp.where(lv == 0, s_1[rs, :], jnp.where(lv == -1, s_d[rs, :], 0.0)))
            scs.append(sc)

        inters = []
        for h, qb, kb, v, b2 in zip(heads, qbs, kbs, vs, b2s):
            b_last = b2[c - 1:c, :]
            st = st_ref[h]
            inters.append(_dot_nt(qb * jnp.exp2(b2).astype(BF16), st.astype(BF16)))
            k_end = kb * jnp.exp2(b_last - b2).astype(BF16)
            st_ref[h] = jnp.exp2(b_last) * st + _dot_tn(v, k_end)

        for li, half in enumerate(halves[:-1]):
            level = n_lv - 1 - li
            blk = 2 * half
            prods = []
            for q, k, qb, kb, b2 in zip(qs, ks, qbs, kbs, b2s):
                if half >= 2 * SUBLANES:
                    lhs, rhs = [], []
                    for r0 in range(0, c, blk):
                        bp = b2[r0 + half - 1:r0 + half, :]
                        lo = slice(r0, r0 + half)
                        up = slice(r0 + half, r0 + blk)
                        lhs.append(qb[up, :] * jnp.exp2(b2[up, :] - bp).astype(BF16))
                        rhs.append(kb[lo, :] * jnp.exp2(bp - b2[lo, :]).astype(BF16))
                        rhs.append(kb[up, :])
                    lhs = lhs[0] if len(lhs) == 1 else jnp.concatenate(lhs, axis=0)
                    prods.append(_dot_nt(lhs, jnp.concatenate(rhs, axis=0)))
                elif half >= SUBLANES:
                    lhs, rhs = [], []
                    for r0 in range(0, c, blk):
                        bp = b2[r0 + half - 1:r0 + half, :]
                        lo = slice(r0, r0 + half)
                        up = slice(r0 + half, r0 + blk)
                        lhs.append(q[up, :] * jnp.exp2(b2[up, :] - bp))
                        rhs.append(k[lo, :] * jnp.exp2(bp - b2[lo, :]))
                        rhs.append(k[up, :])
                    lhs = lhs[0] if len(lhs) == 1 else jnp.concatenate(lhs, axis=0)
                    prods.append(_dot_nt(lhs.astype(BF16), jnp.concatenate(rhs, axis=0).astype(BF16)))
                else:
                    e = jnp.exp2(_neg_abs(b2 - _boundary_rows_in_vreg(b2, half))).astype(BF16)
                    prods.append(_dot_nt(qb * e, kb * e))
            if half >= SUBLANES:
                row_of = [r // SUBLANES for r0 in range(0, c, blk) for r in range(r0 + half, r0 + blk, SUBLANES)]
            else:
                row_of = list(range(nv))
            for sc, s_l in zip(scs, prods):
                for i, r in enumerate(row_of):
                    rs = slice(r * SUBLANES, (r + 1) * SUBLANES)
                    piece = s_l[i * SUBLANES:(i + 1) * SUBLANES, :]
                    sc[r] = jnp.where(lvl_ref[rs, :] == level, piece, sc[r])

        outs = []
        for sc, v, inter in zip(scs, vs, inters):
            scores = jnp.concatenate(sc, axis=0).astype(BF16)
            outs.append(inter + jnp.dot(scores, v, preferred_element_type=F32))
        for o, cl in zip(outs, cols):
            ms = jnp.mean(o * o, axis=-1, keepdims=True)
            o = o * lax.rsqrt(ms + EPS) * nw_ref[:, cl]
            o_ref[rows, cl] = (o * _silu_of_half(g_ref[rows, cl].astype(F32))).astype(o_ref.dtype)
        return carry

    lax.fori_loop(0, n_chunks * n_groups, head_group, 0)


def _hgrn(proj, lb, norm_w, batch, seq, width, ts):
    t = proj.shape[0]
    n_heads = width // HEAD_DIM
    ts = _tile(seq, ts, HG_CHUNK)
    n_chunks = ts // HG_CHUNK
    steps = seq // ts
    row_map = lambda col: (lambda b, s: (b * steps + s, col))
    body = functools.partial(_hgrn_body, n_heads=n_heads, n_chunks=n_chunks)
    return pl.pallas_call(
        body,
        grid=(batch, steps),
        in_specs=[pl.BlockSpec((ts, width), row_map(0)),
                  pl.BlockSpec((ts, width), row_map(1)),
                  pl.BlockSpec((ts, width), row_map(2)),
                  pl.BlockSpec((ts, width), row_map(3)),
                  pl.BlockSpec((1, width), lambda b, s: (0, 0)),
                  pl.BlockSpec((1, width), lambda b, s: (0, 0))],
        out_specs=pl.BlockSpec((ts, width), row_map(0)),
        out_shape=jax.ShapeDtypeStruct((t, width), BF16),
        scratch_shapes=[pltpu.VMEM((n_heads, HEAD_DIM, HEAD_DIM), F32),
                        pltpu.VMEM((HG_CHUNK, HG_CHUNK), jnp.int32)],
        compiler_params=_cparams(("parallel", "arbitrary")),
        name="hgrn2",
    )(proj, proj, proj, proj, lb, norm_w)


def _lru_body(x_ref, y_ref, cw_ref, cb_ref, wg_ref, ba_ref, bx_ref, lam_ref, nw_ref,
              o_ref, xs_ref, h_ref, *, n_blocks, conv_w):
    tl, width = x_ref.shape
    bd = width // n_blocks
    halo = SUBLANES
    first = pl.program_id(1) == 0

    @pl.when(first)
    def _():
        xs_ref[0:halo, :] = jnp.zeros((halo, width), F32)
        h_ref[...] = jnp.zeros_like(h_ref)

    xs_ref[halo:, :] = x_ref[...].astype(F32)
    xb = cb_ref[...] + xs_ref[halo:, :] * cw_ref[conv_w - 1:conv_w, :]
    for j in range(conv_w - 1):
        d = conv_w - 1 - j
        xb = xb + xs_ref[halo - d:halo - d + tl, :] * cw_ref[j:j + 1, :]
    xs_ref[0:halo, :] = xs_ref[tl:tl + halo, :]

    xbb = xb.astype(BF16)
    pre = [jnp.dot(xbb[:, n * bd:(n + 1) * bd], wg_ref[n], preferred_element_type=F32)
           for n in range(n_blocks)]
    pre_a = jnp.concatenate([p[:, :bd] for p in pre], axis=1)
    pre_x = jnp.concatenate([p[:, bd:] for p in pre], axis=1)
    r = _sigmoid(pre_a + ba_ref[...])
    gate_i = _sigmoid_of_half(pre_x + bx_ref[...])
    lam = lam_ref[...]
    softplus = jnp.maximum(-lam, 0.0) + jnp.log(1.0 + jnp.exp(-jnp.abs(lam)))
    neg_log_a = r * (LRU_C * softplus)
    a = jnp.exp2(r * ((-LRU_C * LOG2_E) * softplus))
    z = jnp.tanh(neg_log_a) * (1.0 + a * a)
    mult = z * lax.rsqrt(jnp.maximum(z, F32_MIN_NORMAL))
    row = lax.broadcasted_iota(jnp.int32, (SUBLANES, width), 0)
    head = jnp.where(jnp.logical_and(first, row == 0), 1.0, mult[:SUBLANES, :])
    mult = jnp.concatenate([head, mult[SUBLANES:, :]], axis=0)
    u = xb * gate_i * mult

    ng = tl // SUBLANES
    a3 = a.reshape(ng, SUBLANES, width)
    u3 = u.reshape(ng, SUBLANES, width)
    sub = lax.broadcasted_iota(jnp.int32, a3.shape, 1)
    d = 1
    while d < SUBLANES:
        u3 = a3 * jnp.where(sub < d, 0.0, pltpu.roll(u3, d, axis=1)) + u3
        a3 = a3 * jnp.where(sub < d, 1.0, pltpu.roll(a3, d, axis=1))
        d *= 2
    h_prev = h_ref[0:1, :]
    hs = []
    for gi in range(ng):
        hg = u3[gi] + a3[gi] * h_prev
        hs.append(hg)
        h_prev = hg[SUBLANES - 1:SUBLANES, :]
    h_ref[...] = jnp.broadcast_to(h_prev, h_ref.shape)
    h = jnp.concatenate(hs, axis=0)

    hy = y_ref[...].astype(F32)
    inner = hy * (2.0 * GELU_C + (8.0 * GELU_C * 0.044715) * (hy * hy))
    out = h * (hy + hy * jnp.tanh(inner))
    ms = jnp.mean(out * out, axis=-1, keepdims=True)
    o_ref[...] = (out * lax.rsqrt(ms + EPS) * nw_ref[...]).astype(o_ref.dtype)


def _lru(proj, conv_w, conv_b, w_gate, ba, bx, lam, norm_w, batch, seq, width, col0, tl):
    t = proj.shape[0]
    tl = _tile(seq, tl, SUBLANES)
    steps = seq // tl
    n_blocks = w_gate.shape[0]
    row_map = lambda col: (lambda b, s: (b * steps + s, col))
    vec = lambda r: pl.BlockSpec((r, width), lambda b, s: (0, 0))
    body = functools.partial(_lru_body, n_blocks=n_blocks, conv_w=conv_w.shape[0])
    return pl.pallas_call(
        body,
        grid=(batch, steps),
        in_specs=[pl.BlockSpec((tl, width), row_map(col0)),
                  pl.BlockSpec((tl, width), row_map(col0 + 1)),
                  vec(conv_w.shape[0]), vec(1),
                  pl.BlockSpec(w_gate.shape, lambda b, s: (0, 0, 0)),
                  vec(1), vec(1), vec(1), vec(1)],
        out_specs=pl.BlockSpec((tl, width), row_map(0)),
        out_shape=jax.ShapeDtypeStruct((t, width), BF16),
        scratch_shapes=[pltpu.VMEM((SUBLANES + tl, width), F32), pltpu.VMEM((SUBLANES, width), F32)],
        compiler_params=_cparams(("parallel", "arbitrary")),
        name="rglru",
    )(proj, proj, conv_w, conv_b, w_gate, ba, bx, lam, norm_w)


def _out_proj_body(a_ref, b_ref, wa_ref, wb_ref, x_ref, lnw_ref, h_ref, hn_ref):
    acc = jnp.dot(a_ref[...], wa_ref[...], preferred_element_type=F32)
    acc = acc + jnp.dot(b_ref[...], wb_ref[...], preferred_element_type=F32)
    h = x_ref[...] + acc
    h_ref[...] = h
    ms = jnp.mean(h * h, axis=-1, keepdims=True)
    hn_ref[...] = (h * lax.rsqrt(ms + EPS) * lnw_ref[...]).astype(hn_ref.dtype)


def _out_proj(o_hg, o_lru, w_out, x, ln_w, tm):
    m, ka = o_hg.shape
    kb = o_lru.shape[1]
    n = w_out.shape[1]
    tm = _tile(m, tm, 2 * SUBLANES)
    assert ka == kb
    resident = lambda blk: pl.BlockSpec((ka, n), lambda i: (blk, 0), pipeline_mode=pl.Buffered(1))
    return pl.pallas_call(
        _out_proj_body,
        grid=(m // tm,),
        in_specs=[pl.BlockSpec((tm, ka), lambda i: (i, 0)),
                  pl.BlockSpec((tm, kb), lambda i: (i, 0)),
                  resident(0), resident(1),
                  pl.BlockSpec((tm, n), lambda i: (i, 0)),
                  pl.BlockSpec((1, n), lambda i: (0, 0))],
        out_specs=[pl.BlockSpec((tm, n), lambda i: (i, 0)), pl.BlockSpec((tm, n), lambda i: (i, 0))],
        out_shape=[jax.ShapeDtypeStruct((m, n), F32), jax.ShapeDtypeStruct((m, n), BF16)],
        compiler_params=_cparams(("arbitrary",)),
        name="out_proj",
    )(o_hg, o_lru, w_out, w_out, x, ln_w.reshape(1, n).astype(F32))


def _ffn_up_body(x_ref, wg_ref, wv_ref, cg_ref, cv_ref, bg_ref, bv_ref, wd_ref, o_ref, wdb_ref,
                 ag_ref, av_ref, ao_ref, xs_ref, *, row_tiles, conv_w):
    tf = wg_ref.shape[1]
    halo = FFN_HALO
    wdb_ref[...] = wd_ref[...].astype(wdb_ref.dtype)

    @pl.when(pl.program_id(1) == 0)
    def _():
        xs_ref[...] = x_ref[...]

    n_slab = tf // LANES
    ag_ref[:, 0:halo, :] = jnp.zeros((n_slab, halo, LANES), F32)
    av_ref[:, 0:halo, :] = jnp.zeros((n_slab, halo, LANES), F32)

    def conv(acc_ref, s, r0, half, cw_ref, b_ref):
        lanes = slice(s * LANES, (s + 1) * LANES)
        reads = {o: acc_ref[s, pl.ds(halo + r0 + o, half, stride=2), :] for o in range(1 - conv_w, 2)}
        y_even = b_ref[:, lanes]
        y_odd = b_ref[:, lanes]
        for d in range(conv_w):
            w = cw_ref[conv_w - 1 - d:conv_w - d, lanes]
            y_even = y_even + reads[-d] * w
            y_odd = y_odd + reads[1 - d] * w
        return y_even, y_odd

    r0 = 0
    for rd in row_tiles:
        half = rd // 2
        xs = xs_ref[r0:r0 + rd, :]
        up_g = jnp.dot(xs, wg_ref[...], preferred_element_type=F32)
        up_v = jnp.dot(xs, wv_ref[...], preferred_element_type=F32)
        for s in range(n_slab):
            ag_ref[s, halo + r0:halo + r0 + rd, :] = up_g[:, s * LANES:(s + 1) * LANES]
            av_ref[s, halo + r0:halo + r0 + rd, :] = up_v[:, s * LANES:(s + 1) * LANES]
        for s in range(n_slab):
            g_even, g_odd = conv(ag_ref, s, r0, half, cg_ref, bg_ref)
            v_even, v_odd = conv(av_ref, s, r0, half, cv_ref, bv_ref)
            ao_ref[s, pl.ds(r0, half, stride=2), :] = _silu_of_half(g_even) * v_even
            ao_ref[s, pl.ds(r0 + 1, half, stride=2), :] = _silu_of_half(g_odd) * v_odd
            o_ref[r0:r0 + rd, s * LANES:(s + 1) * LANES] = ao_ref[s, r0:r0 + rd, :].astype(o_ref.dtype)
        r0 += rd


def _ffn_row_tiles(seq, rows=FFN_ROW_TILE):
    rows = _tile(seq, rows, 2 * SUBLANES)
    return (rows,) * (seq // rows)


def _ffn_up(hn, w_up, conv_w, conv_b, w_down, batch, seq, d_ff, tf, row_tiles):
    t, d = hn.shape
    tf = _tile(d_ff, tf, LANES)
    nf = d_ff // tf
    assert sum(row_tiles) == seq and all(r % (2 * SUBLANES) == 0 for r in row_tiles)
    cw = conv_w.shape[0]
    body = functools.partial(_ffn_up_body, row_tiles=row_tiles, conv_w=cw)
    cast_spec = _cast_spec(w_down, nf, batch * nf)
    return pl.pallas_call(
        body,
        grid=(batch, nf),
        in_specs=[pl.BlockSpec((seq, d), lambda b, j: (b, 0), pipeline_mode=pl.Buffered(1)),
                  pl.BlockSpec((d, tf), lambda b, j: (0, j)),
                  pl.BlockSpec((d, tf), lambda b, j: (0, j + nf)),
                  pl.BlockSpec((cw, tf), lambda b, j: (0, j)),
                  pl.BlockSpec((cw, tf), lambda b, j: (0, j + nf)),
                  pl.BlockSpec((1, tf), lambda b, j: (0, j)),
                  pl.BlockSpec((1, tf), lambda b, j: (0, j + nf)),
                  cast_spec],
        out_specs=[pl.BlockSpec((seq, tf), lambda b, j: (b, j)), cast_spec],
        out_shape=[jax.ShapeDtypeStruct((t, d_ff), BF16), jax.ShapeDtypeStruct(w_down.shape, BF16)],
        scratch_shapes=[pltpu.VMEM((tf // LANES, FFN_HALO + seq, LANES), F32),
                        pltpu.VMEM((tf // LANES, FFN_HALO + seq, LANES), F32),
                        pltpu.VMEM((tf // LANES, seq, LANES), F32),
                        pltpu.VMEM((seq, d), BF16)],
        compiler_params=_cparams(("arbitrary", "arbitrary")),
        name="ffn_up",
    )(hn, w_up, w_up, conv_w, conv_w, conv_b, conv_b, w_down)


def _ffn_down_body(a_ref, w_ref, h_ref, o_ref):
    o_ref[...] = h_ref[...] + jnp.dot(a_ref[...], w_ref[...], preferred_element_type=F32)


def _ffn_down(act, w_down, h, tm, tn):
    m, k = act.shape
    n = w_down.shape[1]
    tm, tn = _tile(m, tm, SUBLANES), _tile(n, tn, LANES)
    return pl.pallas_call(
        _ffn_down_body,
        grid=(n // tn, m // tm),
        in_specs=[pl.BlockSpec((tm, k), lambda j, i: (i, 0)),
                  pl.BlockSpec((k, tn), lambda j, i: (0, j), pipeline_mode=pl.Buffered(1)),
                  pl.BlockSpec((tm, tn), lambda j, i: (i, j))],
        out_specs=pl.BlockSpec((tm, tn), lambda j, i: (i, j)),
        out_shape=jax.ShapeDtypeStruct((m, n), F32),
        compiler_params=_cparams(("parallel", "arbitrary")),
        name="ffn_down",
    )(act, w_down, h)


def kernel(x, ln1_w, w_in, lb_gamma, hg_norm_w, lru_conv_w, lru_conv_b, lru_wa, lru_ba, lru_wx, lru_bx,
           lru_lambda, lru_norm_w, w_out, ln2_w, ffn_w_up, ffn_conv_w, ffn_conv_b, ffn_w_down,
           final_norm_w):
    batch, seq, d = x.shape
    depth = ln1_w.shape[0]
    hg_w = hg_norm_w.shape[1]
    lru_w = lru_lambda.shape[1]
    d_ff = ffn_w_down.shape[1]
    assert hg_w == lru_w and w_in.shape[2] == 4 * hg_w + 2 * lru_w
    t = batch * seq
    row = lambda v: v.reshape(1, -1).astype(F32)

    lb_all = jnp.cumsum(jax.nn.softmax(lb_gamma.astype(F32), axis=0), axis=0)
    h = x.reshape(t, d)
    for l in range(depth):
        in_scale = jnp.concatenate([jnp.full((hg_w,), s, F32) for s in (0.5, 1.0, 1.0, 0.5)]
                                   + [jnp.full((lru_w,), s, F32) for s in (1.0, 0.5)])
        w_in_b = (w_in[l].astype(F32) * in_scale).astype(BF16)
        hn = _rmsnorm(h, ln1_w[l], BF16, tm=NORM_ROWS)
        proj, (w_out_b, w_up_b) = _matmul(hn, w_in_b, BF16, *IN_PROJ_TILE,
                                          casts=(w_out[l].astype(F32), ffn_w_up[l].astype(F32)))
        o_hg = _hgrn(proj, row(lb_all[l]), row(hg_norm_w[l]), batch, seq, hg_w, ts=HGRN_ROWS)
        w_gate = jnp.concatenate([lru_wa[l].astype(F32), 0.5 * lru_wx[l].astype(F32)], axis=-1).astype(BF16)
        o_lru = _lru(proj, lru_conv_w[l].astype(F32), row(lru_conv_b[l]), w_gate, row(lru_ba[l]),
                     0.5 * row(lru_bx[l]), row(lru_lambda[l]), row(lru_norm_w[l]), batch, seq, lru_w,
                     col0=4, tl=LRU_ROWS)
        h, hn = _out_proj(o_hg, o_lru, w_out_b, h, ln2_w[l], tm=OUT_PROJ_ROWS)
        gate_scale = jnp.concatenate([jnp.full((d_ff,), 0.5, F32), jnp.ones((d_ff,), F32)])
        act, w_down_b = _ffn_up(hn, w_up_b, ffn_conv_w[l].astype(F32) * gate_scale,
                                row(ffn_conv_b[l]) * gate_scale, ffn_w_down[l].astype(F32),
                                batch, seq, d_ff, tf=FFN_COL_TILE, row_tiles=_ffn_row_tiles(seq))
        h = _ffn_down(act, w_down_b, h, *FFN_DOWN_TILE)
    out = _rmsnorm(h, final_norm_w, x.dtype, tm=NORM_ROWS)
    return out.reshape(batch, seq, d)
```
